```python
import jax
import jax.numpy as jnp
from jax import lax
import numpy as np

D_MODEL = 1024
BATCH = 32
SEQ = 2048
DEPTH = 2
DEC_BATCH = 8
DEC_SEQ = 16
PAST_LEN = 4096

CHUNK = 64
N_MEM = 256
EPS = 1e-6

A_HEADS = 8
A_HEAD_DIM = 64
A_WIDTH = A_HEADS * A_HEAD_DIM
A_PAST_CHUNKS = 8
A_WINDOW = A_PAST_CHUNKS * CHUNK
A_BAND = A_WINDOW + CHUNK
A_MAX_REL = 128
A_N_REL = 2 * A_MAX_REL + 1

B_HEADS = 4
B_HEAD_DIM = 128
B_WIDTH = B_HEADS * B_HEAD_DIM
B_CONV = 4

C_HEADS = 16
C_HEAD_DIM = 64
C_INNER = C_HEADS * C_HEAD_DIM
C_GROUPS = 2
C_STATE = 128
C_XBC = C_INNER + 2 * C_GROUPS * C_STATE
C_CONV = 4

X_HEADS = 4
X_HEAD_DIM = D_MODEL // X_HEADS

D_FF = 2816
F_CONV = 3

IN_SPLITS = (3 * A_WIDTH, 3 * B_WIDTH, B_HEADS, B_HEADS, B_WIDTH, C_INNER, C_XBC, C_HEADS, 3 * D_MODEL)
N_IN = 3 * A_WIDTH + 3 * B_WIDTH + 2 * B_HEADS + B_WIDTH + C_INNER + C_XBC + C_HEADS + 3 * D_MODEL

kernel_name = 'hybrid_chunk_stream_encoder_step'


def rmsnorm(x, g):
    xf = x.astype(jnp.float32)
    y = xf * lax.rsqrt(jnp.mean(xf * xf, axis=-1, keepdims=True) + EPS)
    return (y * g.astype(jnp.float32)).astype(x.dtype)


def l2norm(x):
    xf = x.astype(jnp.float32)
    return xf * lax.rsqrt(jnp.sum(xf * xf, axis=-1, keepdims=True) + EPS)


def split_cols(z, sizes):
    idx = np.cumsum(np.array(sizes))[:-1].tolist()
    return jnp.split(z, idx, axis=-1)


def causal_dwconv(x, buf, w, b=None):
    k_w = w.shape[0]
    L = x.shape[1]
    xp = jnp.concatenate([buf.astype(x.dtype), x], axis=1)
    y = xp[:, 0:L] * w[0]
    for i in range(1, k_w):
        y = y + xp[:, i:i + L] * w[i]
    if b is not None:
        y = y + b
    return y, xp[:, xp.shape[1] - (k_w - 1):]


def scan_chunks(step, state, xs):
    L = xs[0].shape[1]
    cl = min(CHUNK, L)
    nc = L // cl

    def to_chunks(t):
        return jnp.swapaxes(t.reshape((t.shape[0], nc, cl) + t.shape[2:]), 0, 1)

    state, ys = lax.scan(lambda s, c: step(s, *c), state, tuple(to_chunks(t) for t in xs))
    ys = jnp.swapaxes(ys, 0, 1)
    return state, ys.reshape((ys.shape[0], L) + ys.shape[3:])


def rel_bias_lookup(table, rel):
    idx = jnp.clip(rel, -A_MAX_REL, A_MAX_REL) + A_MAX_REL
    return jnp.transpose(table[idx], (2, 0, 1)).astype(jnp.float32)


def band_attn_prompt(q, k, v, rel_table):
    bsz, L, H, Dh = q.shape
    nc = L // CHUNK
    kp = jnp.pad(k, ((0, 0), (A_WINDOW, 0), (0, 0), (0, 0)))
    vp = jnp.pad(v, ((0, 0), (A_WINDOW, 0), (0, 0), (0, 0)))
    qc = q.reshape(bsz, nc, CHUNK, H, Dh)
    rel = jnp.arange(CHUNK)[:, None] + A_WINDOW - jnp.arange(A_BAND)[None, :]
    bias = rel_bias_lookup(rel_table, rel)
    scale = A_HEAD_DIM ** -0.5
    neg = jnp.finfo(jnp.float32).min

    def one_chunk(c):
        start = c * CHUNK
        qb = lax.dynamic_index_in_dim(qc, c, axis=1, keepdims=False)
        kb = lax.dynamic_slice_in_dim(kp, start, A_BAND, axis=1)
        vb = lax.dynamic_slice_in_dim(vp, start, A_BAND, axis=1)
        valid = (start - A_WINDOW + jnp.arange(A_BAND)) >= 0
        s = jnp.einsum('bqhd,bkhd->bhqk', qb, kb).astype(jnp.float32) * scale + bias
        s = jnp.where(valid, s, neg)
        p = jax.nn.softmax(s, axis=-1).astype(vb.dtype)
        return jnp.einsum('bhqk,bkhd->bqhd', p, vb)

    out = lax.map(one_chunk, jnp.arange(nc))
    return jnp.swapaxes(out, 0, 1).reshape(bsz, L, H * Dh)


def band_attn_sample(q, k, v, k_cache, v_cache, rel_table):
    bsz, L, H, Dh = q.shape
    lc = k_cache.shape[1]
    kk = jnp.concatenate([k_cache.astype(k.dtype), k], axis=1)
    vv = jnp.concatenate([v_cache.astype(v.dtype), v], axis=1)
    kpos = jnp.concatenate([jnp.arange(lc) - lc, jnp.arange(L)])
    rel = jnp.arange(L)[:, None] - kpos[None, :]
    bias = rel_bias_lookup(rel_table, rel)
    s = jnp.einsum('bqhd,bkhd->bhqk', q, kk).astype(jnp.float32) * (A_HEAD_DIM ** -0.5) + bias
    p = jax.nn.softmax(s, axis=-1).astype(vv.dtype)
    return jnp.einsum('bhqk,bkhd->bqhd', p, vv).reshape(bsz, L, H * Dh)


def gdn_chunk(s0, q, k, v, beta, g):
    L = q.shape[1]
    incl = jnp.tril(jnp.ones((L, L), dtype=bool))
    strict = jnp.tril(jnp.ones((L, L), dtype=bool), -1)
    gc = jnp.cumsum(g, axis=1).transpose(0, 2, 1)
    decay = jnp.exp(jnp.where(incl, gc[..., :, None] - gc[..., None, :], -jnp.inf))
    qh = q.transpose(0, 2, 1, 3)
    kh = k.transpose(0, 2, 1, 3)
    vh = v.transpose(0, 2, 1, 3)
    bh = beta.transpose(0, 2, 1)[..., None]
    kb = kh * bh
    lmat = jnp.where(strict, jnp.einsum('bhik,bhjk->bhij', kb, kh) * decay, 0.0)
    tmat = lmat + jnp.eye(L, dtype=lmat.dtype)
    rhs = jnp.concatenate([vh * bh, kb * jnp.exp(gc)[..., None]], axis=-1)
    sol = lax.linalg.triangular_solve(tmat, rhs, left_side=True, lower=True, unit_diagonal=True)
    w_v, w_k = sol[..., :B_HEAD_DIM], sol[..., B_HEAD_DIM:]
    u = w_v - jnp.einsum('bhlk,bhkv->bhlv', w_k, s0)
    qk = jnp.einsum('bhik,bhjk->bhij', qh, kh) * decay
    o = jnp.einsum('bhlk,bhkv->bhlv', qh * jnp.exp(gc)[..., None], s0) + jnp.einsum('bhij,bhjv->bhiv', qk, u)
    g_last = gc[..., -1:]
    s_new = s0 * jnp.exp(g_last)[..., None] + jnp.einsum('bhlk,bhlv->bhkv', kh * jnp.exp(g_last - gc)[..., None], u)
    return s_new, o.transpose(0, 2, 1, 3)


def ssd_chunk(h0, x, dt, bm, cm, a_neg):
    bsz, L = x.shape[0], x.shape[1]
    R = C_HEADS // C_GROUPS
    incl = jnp.tril(jnp.ones((L, L), dtype=bool))
    ac = jnp.cumsum(dt * a_neg, axis=1).transpose(0, 2, 1).reshape(bsz, C_GROUPS, R, L)
    decay = jnp.exp(jnp.where(incl, ac[..., :, None] - ac[..., None, :], -jnp.inf))
    xdt = (x * dt[..., None]).reshape(bsz, L, C_GROUPS, R, C_HEAD_DIM)
    cb = jnp.einsum('bign,bjgn->bgij', cm, bm)
    y_diag = jnp.einsum('bgrij,bjgrp->bigrp', cb[:, :, None] * decay, xdt)
    h0g = h0.reshape(bsz, C_GROUPS, R, C_HEAD_DIM, C_STATE)
    y_off = jnp.einsum('bign,bgrpn->bigrp', cm, h0g) * jnp.exp(ac).transpose(0, 3, 1, 2)[..., None]
    a_last = ac[..., -1:]
    h_new = h0g * jnp.exp(a_last)[..., None] + jnp.einsum('bjgn,bgrj,bjgrp->bgrpn', bm, jnp.exp(a_last - ac), xdt)
    y = (y_diag + y_off).reshape(bsz, L, C_HEADS, C_HEAD_DIM)
    return h_new.reshape(bsz, C_HEADS, C_HEAD_DIM, C_STATE), y


def memory_kv(mem, g, wk, wv):
    m = rmsnorm(mem, g)
    bsz = m.shape[0]
    k = (m @ wk).reshape(bsz, N_MEM, X_HEADS, X_HEAD_DIM)
    v = (m @ wv).reshape(bsz, N_MEM, X_HEADS, X_HEAD_DIM)
    return k, v


def memory_attn(h, mk, mv, wq, wo):
    bsz, L, _ = h.shape
    q = (h @ wq).reshape(bsz, L, X_HEADS, X_HEAD_DIM)
    s = jnp.einsum('bqhd,bkhd->bhqk', q, mk.astype(q.dtype)).astype(jnp.float32) * (X_HEAD_DIM ** -0.5)
    p = jax.nn.softmax(s, axis=-1).astype(h.dtype)
    o = jnp.einsum('bhqk,bkhd->bqhd', p, mv.astype(h.dtype)).reshape(bsz, L, D_MODEL)
    return o @ wo


def encoder_layer(x, lw, a_past, b_conv, b_rec, c_conv, c_ssm, f_conv, mem_k, mem_v):
    f32 = jnp.float32
    bsz, L, _ = x.shape
    h = rmsnorm(x, lw['norm_mix'])
    a_qkv, b_qkv, b_beta, b_dec, b_gate, c_z, c_xbc, c_dt, gates = split_cols(h @ lw['w_in'], IN_SPLITS)

    aq, ak, av = [t.reshape(bsz, L, A_HEADS, A_HEAD_DIM) for t in jnp.split(a_qkv, 3, axis=-1)]
    if a_past is None:
        ya = band_attn_prompt(aq, ak, av, lw['a_rel_bias'])
        keep = min(A_WINDOW, L)
        a_k_new, a_v_new = ak[:, L - keep:], av[:, L - keep:]
    else:
        ya = band_attn_sample(aq, ak, av, a_past[0], a_past[1], lw['a_rel_bias'])
        a_k_new, a_v_new = ak, av

    bqkv, b_conv_new = causal_dwconv(b_qkv, b_conv, lw['b_conv_w'])
    bqkv = jax.nn.silu(bqkv)
    bq, bk, bv = [t.reshape(bsz, L, B_HEADS, B_HEAD_DIM) for t in jnp.split(bqkv, 3, axis=-1)]
    bq = l2norm(bq) * (B_HEAD_DIM ** -0.5)
    bk = l2norm(bk)
    beta = jax.nn.sigmoid(b_beta.astype(f32))
    gdec = -jnp.exp(lw['b_a_log'].astype(f32)) * jax.nn.softplus(b_dec.astype(f32) + lw['b_dt_bias'].astype(f32))
    b_rec_new, ob = scan_chunks(gdn_chunk, b_rec.astype(f32), (bq, bk, bv.astype(f32), beta, gdec))
    ob = rmsnorm(ob.astype(x.dtype), lw['b_norm'])
    yb = ob.reshape(bsz, L, B_WIDTH) * jax.nn.silu(b_gate)

    xbc, c_conv_new = causal_dwconv(c_xbc, c_conv, lw['c_conv_w'], lw['c_conv_b'])
    xbc = jax.nn.silu(xbc)
    cx, cbm, ccm = split_cols(xbc, (C_INNER, C_GROUPS * C_STATE, C_GROUPS * C_STATE))
    cx = cx.reshape(bsz, L, C_HEADS, C_HEAD_DIM).astype(f32)
    cbm = cbm.reshape(bsz, L, C_GROUPS, C_STATE).astype(f32)
    ccm = ccm.reshape(bsz, L, C_GROUPS, C_STATE).astype(f32)
    dt = jax.nn.softplus(c_dt.astype(f32) + lw['c_dt_bias'].astype(f32))
    a_neg = -jnp.exp(lw['c_a_log'].astype(f32))
    c_ssm_new, yc = scan_chunks(lambda s, xx, dd, bb, cc: ssd_chunk(s, xx, dd, bb, cc, a_neg), c_ssm.astype(f32), (cx, dt, cbm, ccm))
    yc = yc + lw['c_d'].astype(f32)[:, None] * cx
    yc = rmsnorm(yc.astype(x.dtype).reshape(bsz, L, C_INNER) * jax.nn.silu(c_z), lw['c_norm'])

    ga, gb, gc = jnp.split(jax.nn.sigmoid(gates), 3, axis=-1)
    m = ga * (ya @ lw['w_br_a']) + gb * (yb @ lw['w_br_b']) + gc * (yc @ lw['w_br_c'])
    x = x + m @ lw['w_out']

    x = x + memory_attn(rmsnorm(x, lw['norm_x']), mem_k, mem_v, lw['wx_q'], lw['wx_o'])

    hf = rmsnorm(x, lw['norm_ffn'])
    u, gf = jnp.split(hf @ lw['w_up'], 2, axis=-1)
    gf, f_conv_new = causal_dwconv(gf, f_conv, lw['f_conv_w'], lw['f_conv_b'])
    x = x + (u * jax.nn.silu(gf)) @ lw['w_down']
    return x, (a_k_new, a_v_new, b_conv_new, b_rec_new, c_conv_new, c_ssm_new, f_conv_new)


def setup_inputs(seed: int = 0) -> dict:
    key = jax.random.key(seed)
    keys = jax.random.split(key, 64)
    cnt = [0]

    def nk():
        k = keys[cnt[0]]
        cnt[0] += 1
        return k

    def nrm(shape, scale):
        return jax.random.normal(nk(), shape, jnp.float32) * scale

    def gain(shape):
        return 1.0 + nrm(shape, 0.02)

    def dt_bias(n):
        u = jax.random.uniform(nk(), (DEPTH, n), jnp.float32)
        lo, hi = jnp.log(jnp.float32(0.001)), jnp.log(jnp.float32(0.1))
        dtv = jnp.exp(u * (hi - lo) + lo)
        return dtv + jnp.log(-jnp.expm1(-dtv))

    def a_log(n):
        return jnp.log(jax.random.uniform(nk(), (DEPTH, n), jnp.float32, 1.0, 16.0))

    ac = min(A_WINDOW, PAST_LEN)
    d = D_MODEL
    return {
        'x_prompt': nrm((BATCH, SEQ, d), 1.0),
        'x_sample': nrm((DEC_BATCH, DEC_SEQ, d), 1.0),
        'cache_attn_k': nrm((DEPTH, DEC_BATCH, ac, A_HEADS, A_HEAD_DIM), 1.0),
        'cache_attn_v': nrm((DEPTH, DEC_BATCH, ac, A_HEADS, A_HEAD_DIM), 1.0),
        'state_b_conv': nrm((DEPTH, DEC_BATCH, B_CONV - 1, 3 * B_WIDTH), 1.0),
        'state_b_rec': nrm((DEPTH, DEC_BATCH, B_HEADS, B_HEAD_DIM, B_HEAD_DIM), 0.1),
        'state_c_conv': nrm((DEPTH, DEC_BATCH, C_CONV - 1, C_XBC), 1.0),
        'state_c_ssm': nrm((DEPTH, DEC_BATCH, C_HEADS, C_HEAD_DIM, C_STATE), 0.1),
        'state_ffn_conv': nrm((DEPTH, DEC_BATCH, F_CONV - 1, D_FF), 1.0),
        'cache_mem_k': nrm((DEPTH, DEC_BATCH, N_MEM, X_HEADS, X_HEAD_DIM), 1.0),
        'cache_mem_v': nrm((DEPTH, DEC_BATCH, N_MEM, X_HEADS, X_HEAD_DIM), 1.0),
        'mem_prompt': nrm((BATCH, N_MEM, d), 1.0),
        'norm_mix': gain((DEPTH, d)),
        'w_in': nrm((DEPTH, d, N_IN), d ** -0.5),
        'a_rel_bias': nrm((DEPTH, A_N_REL, A_HEADS), 0.5),
        'b_conv_w': nrm((DEPTH, B_CONV, 3 * B_WIDTH), 0.5),
        'b_a_log': a_log(B_HEADS),
        'b_dt_bias': dt_bias(B_HEADS),
        'b_norm': gain((DEPTH, B_HEAD_DIM)),
        'c_conv_w': nrm((DEPTH, C_CONV, C_XBC), 0.5),
        'c_conv_b': nrm((DEPTH, C_XBC), 0.02),
        'c_dt_bias': dt_bias(C_HEADS),
        'c_a_log': a_log(C_HEADS),
        'c_d': gain((DEPTH, C_HEADS)),
        'c_norm': gain((DEPTH, C_INNER)),
        'w_br_a': nrm((DEPTH, A_WIDTH, d), A_WIDTH ** -0.5),
        'w_br_b': nrm((DEPTH, B_WIDTH, d), B_WIDTH ** -0.5),
        'w_br_c': nrm((DEPTH, C_INNER, d), C_INNER ** -0.5),
        'w_out': nrm((DEPTH, d, d), d ** -0.5),
        'norm_x': gain((DEPTH, d)),
        'norm_mem': gain((DEPTH, d)),
        'wx_q': nrm((DEPTH, d, d), d ** -0.5),
        'wx_k': nrm((DEPTH, d, d), d ** -0.5),
        'wx_v': nrm((DEPTH, d, d), d ** -0.5),
        'wx_o': nrm((DEPTH, d, d), d ** -0.5),
        'norm_ffn': gain((DEPTH, d)),
        'w_up': nrm((DEPTH, d, 2 * D_FF), d ** -0.5),
        'f_conv_w': nrm((DEPTH, F_CONV, D_FF), F_CONV ** -0.5),
        'f_conv_b': nrm((DEPTH, D_FF), 0.02),
        'w_down': nrm((DEPTH, D_FF, d), D_FF ** -0.5),
        'norm_final': gain((d,)),
    }


def reference(x_prompt, x_sample, cache_attn_k, cache_attn_v, state_b_conv, state_b_rec, state_c_conv, state_c_ssm, state_ffn_conv, cache_mem_k, cache_mem_v, mem_prompt, norm_mix, w_in, a_rel_bias, b_conv_w, b_a_log, b_dt_bias, b_norm, c_conv_w, c_conv_b, c_dt_bias, c_a_log, c_d, c_norm, w_br_a, w_br_b, w_br_c, w_out, norm_x, norm_mem, wx_q, wx_k, wx_v, wx_o, norm_ffn, w_up, f_conv_w, f_conv_b, w_down, norm_final):
    dtype = x_prompt.dtype
    nb = x_prompt.shape[0]
    xp, xs = x_prompt, x_sample
    p_states, s_states, p_mk, p_mv = [], [], [], []
    for l in range(DEPTH):
        lw = {
            'norm_mix': norm_mix[l], 'w_in': w_in[l], 'a_rel_bias': a_rel_bias[l],
            'b_conv_w': b_conv_w[l], 'b_a_log': b_a_log[l], 'b_dt_bias': b_dt_bias[l], 'b_norm': b_norm[l],
            'c_conv_w': c_conv_w[l], 'c_conv_b': c_conv_b[l], 'c_dt_bias': c_dt_bias[l], 'c_a_log': c_a_log[l],
            'c_d': c_d[l], 'c_norm': c_norm[l],
            'w_br_a': w_br_a[l], 'w_br_b': w_br_b[l], 'w_br_c': w_br_c[l], 'w_out': w_out[l],
            'norm_x': norm_x[l], 'wx_q': wx_q[l], 'wx_o': wx_o[l],
            'norm_ffn': norm_ffn[l], 'w_up': w_up[l], 'f_conv_w': f_conv_w[l], 'f_conv_b': f_conv_b[l], 'w_down': w_down[l],
        }
        mk, mv = memory_kv(mem_prompt, norm_mem[l], wx_k[l], wx_v[l])
        xp, st_p = encoder_layer(
            xp, lw, None,
            jnp.zeros((nb, B_CONV - 1, 3 * B_WIDTH), dtype),
            jnp.zeros((nb, B_HEADS, B_HEAD_DIM, B_HEAD_DIM), jnp.float32),
            jnp.zeros((nb, C_CONV - 1, C_XBC), dtype),
            jnp.zeros((nb, C_HEADS, C_HEAD_DIM, C_STATE), jnp.float32),
            jnp.zeros((nb, F_CONV - 1, D_FF), dtype),
            mk, mv)
        p_states.append(st_p)
        p_mk.append(mk)
        p_mv.append(mv)
        xs, st_s = encoder_layer(
            xs, lw, (cache_attn_k[l], cache_attn_v[l]),
            state_b_conv[l], state_b_rec[l], state_c_conv[l], state_c_ssm[l], state_ffn_conv[l],
            cache_mem_k[l], cache_mem_v[l])
        s_states.append(st_s)

    y_prompt = rmsnorm(xp, norm_final)
    y_sample = rmsnorm(xs, norm_final)
    pst = [jnp.stack([s[i] for s in p_states]).astype(dtype) for i in range(7)]
    sst = [jnp.stack([s[i] for s in s_states]).astype(dtype) for i in range(7)]
    attn_k_prompt, attn_v_prompt, b_conv_prompt, b_rec_prompt, c_conv_prompt, c_ssm_prompt, ffn_conv_prompt = pst
    attn_k_sample, attn_v_sample, b_conv_sample, b_rec_sample, c_conv_sample, c_ssm_sample, ffn_conv_sample = sst
    mem_k_prompt = jnp.stack(p_mk).astype(dtype)
    mem_v_prompt = jnp.stack(p_mv).astype(dtype)
    return (y_prompt, y_sample, attn_k_prompt, attn_v_prompt, b_conv_prompt, b_rec_prompt, c_conv_prompt, c_ssm_prompt, ffn_conv_prompt, mem_k_prompt, mem_v_prompt, attn_k_sample, attn_v_sample, b_conv_sample, b_rec_sample, c_conv_sample, c_ssm_sample, ffn_conv_sample)
```

```python
import functools

import jax
import jax.numpy as jnp
import numpy as np
from jax import lax
from jax.experimental import pallas as pl
from jax.experimental.pallas import tpu as pltpu

F32 = jnp.float32
BF16 = jnp.bfloat16
SDS = jax.ShapeDtypeStruct

D_MODEL = 1024
DEPTH = 2
CHUNK = 64
N_MEM = 256
EPS = 1e-6

A_HEADS = 8
A_HEAD_DIM = 64
A_WIDTH = A_HEADS * A_HEAD_DIM
A_WINDOW = 8 * CHUNK
A_MAX_REL = 128

B_HEADS = 4
B_HEAD_DIM = 128
B_WIDTH = B_HEADS * B_HEAD_DIM
B_CONV = 4

C_HEADS = 16
C_HEAD_DIM = 64
C_INNER = C_HEADS * C_HEAD_DIM
C_GROUPS = 2
C_STATE = 128
C_XBC = C_INNER + 2 * C_GROUPS * C_STATE
C_CONV = 4
C_GROUP_W = C_INNER // C_GROUPS

X_HEADS = 4
X_HEAD_DIM = D_MODEL // X_HEADS

D_FF = 2816
F_CONV = 3

LANES = 128
SUBLANES = 8
NEG = -1e30
ATTN_HEADS_PER_STEP = 4
VMEM_LIMIT = 56 * 1024 * 1024

NZ = 9216
ZW_A = A_WIDTH
ZB_BQKV = 1
ZB_CZ = 3
ZB_CX = 4
ZB_CBC = 10
ZB_BGATE = 11
ZB_GATES = 6
SM_BETA = 0
SM_DEC = 4
SM_DT = 8

_O_BQKV = 3 * A_WIDTH
_O_BETA = _O_BQKV + 3 * B_WIDTH
_O_DEC = _O_BETA + B_HEADS
_O_BGATE = _O_DEC + B_HEADS
_O_CZ = _O_BGATE + B_WIDTH
_O_CXBC = _O_CZ + C_INNER
_O_CDT = _O_CXBC + C_XBC
_O_GATES = _O_CDT + C_HEADS
_PERM_MAIN = np.concatenate([
    np.arange(0, _O_BETA),
    np.arange(_O_CZ, _O_CXBC),
    np.arange(_O_CXBC, _O_CDT),
    np.arange(_O_BGATE, _O_CZ),
    np.arange(_O_GATES, _O_GATES + 3 * D_MODEL),
])
_PERM_SMALL = np.concatenate([np.arange(_O_BETA, _O_BGATE), np.arange(_O_CDT, _O_GATES)])


def _cparams(sem):
    return pltpu.CompilerParams(dimension_semantics=sem, vmem_limit_bytes=VMEM_LIMIT)


def _rms(x, g):
    return x * lax.rsqrt(jnp.mean(x * x, axis=-1, keepdims=True) + EPS) * g


def _silu(x):
    return x * jax.nn.sigmoid(x)


def _softplus(x):
    return jnp.maximum(x, 0.0) + jnp.log1p(jnp.exp(-jnp.abs(x)))


def _dot(a, b):
    return jnp.dot(a, b, preferred_element_type=F32)


def _pieces(a, n):
    out = []
    for _ in range(n - 1):
        p = a.astype(BF16)
        out.append(p)
        a = a - p.astype(F32)
    out.append(a.astype(BF16))
    return out


def _dot_sel(sel, b):
    s16 = sel.astype(BF16)
    b1, b2, b3 = _pieces(b, 3)
    return _dot(s16, b1) + (_dot(s16, b2) + _dot(s16, b3))


def _dot_pick(a, sel):
    s16 = sel.astype(BF16)
    a1, a2, a3 = _pieces(a, 3)
    return _dot(a1, s16) + (_dot(a2, s16) + _dot(a3, s16))


def _dot_x3(a, b):
    ah, al = _pieces(a, 2)
    bh, bl = _pieces(b, 2)
    return _dot(ah, bh) + (_dot(ah, bl) + _dot(al, bh))


def _dot_nt(a, b):
    return lax.dot_general(a, b, (((1,), (1,)), ((), ())), preferred_element_type=F32)


def _dot_tn(a, b):
    return lax.dot_general(a, b, (((0,), (0,)), ((), ())), preferred_element_type=F32)


def _row_start(i, n):
    return i * n if isinstance(i, int) else pl.multiple_of(i * n, n)


def _causal_conv(prev8, cur, w_ref, taps):
    ext = jnp.concatenate([prev8, cur], axis=0)
    acc = cur * w_ref[taps - 1:taps, :]
    for s in range(1, taps):
        acc = acc + pltpu.roll(ext, s, 0)[SUBLANES:, :] * w_ref[taps - 1 - s:taps - s, :]
    return acc, ext[cur.shape[0]:, :]


def _inproj_kernel(x_ref, g_ref, w_ref, ws_ref, z_ref, zs_ref, h_scr):
    @pl.when(pl.program_id(1) == 0)
    def _():
        hb = _rms(x_ref[...], g_ref[...]).astype(BF16)
        h_scr[...] = hb
        zs_ref[...] = _dot(hb, ws_ref[...])

    z_ref[...] = _dot(h_scr[...], w_ref[...]).astype(BF16)


def _in_proj(x2d, g, w_main, w_small):
    T = x2d.shape[0]
    tm = min(1024, T)
    tn = 1536
    return pl.pallas_call(
        _inproj_kernel,
        grid=(T // tm, NZ // tn),
        in_specs=[
            pl.BlockSpec((tm, D_MODEL), lambda i, j: (i, 0)),
            pl.BlockSpec((1, D_MODEL), lambda i, j: (0, 0)),
            pl.BlockSpec((D_MODEL, tn), lambda i, j: (0, j)),
            pl.BlockSpec((D_MODEL, LANES), lambda i, j: (0, 0)),
        ],
        out_specs=[
            pl.BlockSpec((tm, tn), lambda i, j: (i, j)),
            pl.BlockSpec((tm, LANES), lambda i, j: (i, 0)),
        ],
        out_shape=[SDS((T, NZ), BF16), SDS((T, LANES), F32)],
        scratch_shapes=[pltpu.VMEM((tm, D_MODEL), BF16)],
        compiler_params=_cparams(("parallel", "arbitrary")),
        name="in_proj",
    )(x2d, g, w_main, w_small)


def _attn_kernel(*refs, L, C, G, has_past):
    if has_past:
        q_ref, k_ref, v_ref, pk_ref, pv_ref, bias_ref, o_ref, kx, vx = refs
        kx[0:A_WINDOW, :] = pk_ref[0].astype(BF16)
        vx[0:A_WINDOW, :] = pv_ref[0].astype(BF16)
    else:
        q_ref, k_ref, v_ref, bias_ref, o_ref, kx, vx = refs
        kx[0:A_WINDOW, :] = jnp.zeros((A_WINDOW, A_WIDTH), BF16)
        vx[0:A_WINDOW, :] = jnp.zeros((A_WINDOW, A_WIDTH), BF16)
    kx[A_WINDOW:A_WINDOW + L, :] = k_ref[...]
    vx[A_WINDOW:A_WINDOW + L, :] = v_ref[...]

    GC = G * C
    NB = A_WINDOW + GC
    lane = lax.broadcasted_iota(jnp.int32, (1, LANES), 1)
    first_head = lane < A_HEAD_DIM
    col = lax.broadcasted_iota(jnp.int32, (1, NB), 1)
    scale = A_HEAD_DIM ** -0.5

    def group(g, carry, masked):
        r0 = _row_start(g, GC)
        q = q_ref[pl.ds(r0, GC), :] * jnp.asarray(scale, BF16)
        kb = kx[pl.ds(r0, NB), :]
        vb = vx[pl.ds(r0, NB), :]
        for h0 in range(0, A_HEADS, ATTN_HEADS_PER_STEP):
            hs = range(h0, h0 + ATTN_HEADS_PER_STEP)
            sl = {h: slice((h // 2) * LANES, (h // 2 + 1) * LANES) for h in hs}
            msk = {h: first_head if h % 2 == 0 else jnp.logical_not(first_head) for h in hs}
            s = {h: _dot_nt(jnp.where(msk[h], q[:, sl[h]], jnp.zeros((GC, LANES), BF16)), kb[:, sl[h]]) for h in hs}
            s = {h: s[h] + bias_ref[h] for h in hs}
            if masked:
                s = {h: jnp.where(r0 + col >= A_WINDOW, s[h], NEG) for h in hs}
            p = {h: jnp.exp(s[h] - jnp.max(s[h], axis=-1, keepdims=True)) for h in hs}
            l = {h: jnp.sum(p[h], axis=-1, keepdims=True) for h in hs}
            pv = {h: _dot(p[h].astype(BF16), vb[:, sl[h]]) / l[h] for h in hs}
            for h in hs:
                if h % 2 == 1:
                    o_ref[pl.ds(r0, GC), sl[h]] = jnp.where(first_head, pv[h - 1], pv[h]).astype(BF16)
        return carry

    ng = L // GC
    n_masked = 0 if has_past else min(ng, -(-A_WINDOW // GC))
    if ng == 1:
        group(0, 0, n_masked > 0)
    else:
        if n_masked:
            lax.fori_loop(0, n_masked, functools.partial(group, masked=True), 0)
        if ng > n_masked:
            lax.fori_loop(n_masked, ng, functools.partial(group, masked=False), 0)


def _attn_bias(table, C, G):
    GC = G * C
    NB = A_WINDOW + GC
    r = np.arange(GC)[:, None]
    j = np.arange(NB)[None, :]
    rel = np.clip(r + A_WINDOW - j, -A_MAX_REL, A_MAX_REL) + A_MAX_REL
    lo = (r // C) * C
    allowed = (j >= lo) & (j < lo + A_WINDOW + C)
    b = jnp.transpose(table[rel], (2, 0, 1)).astype(F32)
    return jnp.where(allowed[None], b, NEG)


def _band_attn(z, bsz, L, C, G, bias, past):
    T = bsz * L
    has_past = past is not None
    NB = A_WINDOW + G * C
    in_specs = [pl.BlockSpec((L, A_WIDTH), lambda b, i=i: (b, i)) for i in range(3)]
    args = [z, z, z]
    if has_past:
        in_specs += [pl.BlockSpec((1, A_WINDOW, A_WIDTH), lambda b: (b, 0, 0))] * 2
        args += list(past)
    in_specs.append(pl.BlockSpec((A_HEADS, G * C, NB), lambda b: (0, 0, 0)))
    args.append(bias)
    return pl.pallas_call(
        functools.partial(_attn_kernel, L=L, C=C, G=G, has_past=has_past),
        grid=(bsz,),
        in_specs=in_specs,
        out_specs=pl.BlockSpec((L, A_WIDTH), lambda b: (b, 0)),
        out_shape=SDS((T, A_WIDTH), BF16),
        scratch_shapes=[pltpu.VMEM((A_WINDOW + L, A_WIDTH), BF16)] * 2,
        compiler_params=_cparams(("parallel",)),
        name="band_attn",
    )(*args)


def _gdn_kernel(qkv_ref, sm_ref, gate_ref, cst_ref, rst_ref, cw_ref, prm_ref, bn_ref,
                yb_ref, cst_o_ref, rst_o_ref, s_scr, *, L, cl):
    nc = L // cl
    nsq = int(np.log2(cl)) - 1
    s_scr[...] = rst_ref[0]
    bias_row = prm_ref[0:1, :]
    aneg_row = -jnp.exp(prm_ref[1:2, :])
    bn = bn_ref[...]
    ri = lax.broadcasted_iota(jnp.int32, (cl, cl), 0)
    ci = lax.broadcasted_iota(jnp.int32, (cl, cl), 1)
    incl = ri >= ci
    strict = ri > ci
    tril = incl.astype(F32)
    eye = (ri == ci).astype(F32)

    def body(c, prev8):
        r0 = _row_start(c, cl)
        cur = qkv_ref[pl.ds(r0, cl), :].astype(F32)
        conv, new_prev = _causal_conv(prev8, cur, cw_ref, B_CONV)
        act = _silu(conv)
        smc = sm_ref[pl.ds(r0, cl), :]
        beta_all = jax.nn.sigmoid(smc)
        gv = _softplus(smc + bias_row) * aneg_row
        gcs = _dot_sel(tril, gv)
        H = range(B_HEADS)
        hsl = [slice(h * B_HEAD_DIM, (h + 1) * B_HEAD_DIM) for h in H]
        q = [act[:, h * B_HEAD_DIM:(h + 1) * B_HEAD_DIM] for h in H]
        k = [act[:, B_WIDTH + h * B_HEAD_DIM:B_WIDTH + (h + 1) * B_HEAD_DIM] for h in H]
        v = [act[:, 2 * B_WIDTH + h * B_HEAD_DIM:2 * B_WIDTH + (h + 1) * B_HEAD_DIM] for h in H]
        q = [x * lax.rsqrt(jnp.sum(x * x, axis=-1, keepdims=True) + EPS) * (B_HEAD_DIM ** -0.5) for x in q]
        k = [x * lax.rsqrt(jnp.sum(x * x, axis=-1, keepdims=True) + EPS) for x in k]
        beta = [beta_all[:, SM_BETA + h:SM_BETA + h + 1] for h in H]
        g = [gv[:, SM_DEC + h:SM_DEC + h + 1] for h in H]
        gc = [gcs[:, SM_DEC + h:SM_DEC + h + 1] for h in H]
        gl = [gcs[cl - 1:cl, SM_DEC + h:SM_DEC + h + 1] for h in H]
        e = [_dot_sel(tril, jnp.where(strict, g[h], 0.0)) for h in H]
        decay = [jnp.where(incl, jnp.exp(e[h]), 0.0) for h in H]
        kb = [k[h] * beta[h] for h in H]
        k16 = [k[h].astype(BF16) for h in H]
        a = [_dot_nt(kb[h].astype(BF16), k16[h]) for h in H]
        m = [jnp.where(strict, -(a[h] * decay[h]), 0.0) for h in H]
        p = [eye + m[h] for h in H]
        for _ in range(nsq):
            m = [_dot_x3(m[h], m[h]) for h in H]
            p = [p[h] + _dot_x3(p[h], m[h]) for h in H]
        egc = [jnp.exp(gc[h]) for h in H]
        sol = [_dot_x3(p[h], jnp.concatenate([v[h] * beta[h], kb[h] * egc[h]], axis=1)) for h in H]
        qk = [_dot_nt(q[h].astype(BF16), k16[h]) * decay[h] for h in H]
        s0 = [s_scr[h] for h in H]
        s16 = [s0[h].astype(BF16) for h in H]
        u = [sol[h][:, :B_HEAD_DIM] - _dot(sol[h][:, B_HEAD_DIM:].astype(BF16), s16[h]) for h in H]
        u16 = [u[h].astype(BF16) for h in H]
        kt = [(k[h] * jnp.exp(gl[h] - gc[h])).astype(BF16) for h in H]
        snew = [s0[h] * jnp.exp(gl[h]) + _dot_tn(kt[h], u16[h]) for h in H]
        o = [_dot((q[h] * egc[h]).astype(BF16), s16[h]) + _dot(qk[h].astype(BF16), u16[h]) for h in H]
        for h in H:
            s_scr[h] = snew[h]
            gate = gate_ref[pl.ds(r0, cl), hsl[h]].astype(F32)
            yb_ref[pl.ds(r0, cl), hsl[h]] = (_rms(o[h], bn) * _silu(gate)).astype(BF16)
        return new_prev

    prev = cst_ref[0]
    prev = body(0, prev) if nc == 1 else lax.fori_loop(0, nc, body, prev)
    cst_o_ref[0] = prev[SUBLANES - (B_CONV - 1):, :]
    rst_o_ref[0] = s_scr[...]


def _gdn(z, zs, bsz, L, cst, rst, cw, prm, bn):
    T = bsz * L
    cl = min(CHUNK, L)
    return pl.pallas_call(
        functools.partial(_gdn_kernel, L=L, cl=cl),
        grid=(bsz,),
        in_specs=[
            pl.BlockSpec((L, 3 * B_WIDTH), lambda b: (b, ZB_BQKV)),
            pl.BlockSpec((L, LANES), lambda b: (b, 0)),
            pl.BlockSpec((L, B_WIDTH), lambda b: (b, ZB_BGATE)),
            pl.BlockSpec((1, SUBLANES, 3 * B_WIDTH), lambda b: (b, 0, 0)),
            pl.BlockSpec((1, B_HEADS, B_HEAD_DIM, B_HEAD_DIM), lambda b: (b, 0, 0, 0)),
            pl.BlockSpec((B_CONV, 3 * B_WIDTH), lambda b: (0, 0)),
            pl.BlockSpec((SUBLANES, LANES), lambda b: (0, 0)),
            pl.BlockSpec((1, B_HEAD_DIM), lambda b: (0, 0)),
        ],
        out_specs=[
            pl.BlockSpec((L, B_WIDTH), lambda b: (b, 0)),
            pl.BlockSpec((1, B_CONV - 1, 3 * B_WIDTH), lambda b: (b, 0, 0)),
            pl.BlockSpec((1, B_HEADS, B_HEAD_DIM, B_HEAD_DIM), lambda b: (b, 0, 0, 0)),
        ],
        out_shape=[
            SDS((T, B_WIDTH), BF16),
            SDS((bsz, B_CONV - 1, 3 * B_WIDTH), F32),
            SDS((bsz, B_HEADS, B_HEAD_DIM, B_HEAD_DIM), F32),
        ],
        scratch_shapes=[pltpu.VMEM((B_HEADS, B_HEAD_DIM, B_HEAD_DIM), F32)],
        compiler_params=_cparams(("parallel",)),
        name="gdn",
    )(z, zs, z, cst, rst, cw, prm, bn)


def _ssd_kernel(cx_ref, cbc_ref, cz_ref, sm_ref, cst_ref, sst_ref, cwx_ref, cwbc_ref, cbx_ref, cbbc_ref,
                expand_ref, rows_ref, yc_ref, cst_o_ref, sst_o_ref, h_scr, *, L, cl):
    nc = L // cl
    h_scr[...] = sst_ref[0]
    biasx = rows_ref[0:1, :]
    anegx = -jnp.exp(rows_ref[1:2, :])
    cdx = rows_ref[2:3, :]
    cn = rows_ref[3:4, :]
    ri = lax.broadcasted_iota(jnp.int32, (cl, cl), 0)
    ci = lax.broadcasted_iota(jnp.int32, (cl, cl), 1)
    tril = (ri >= ci).astype(F32)
    rx = lax.broadcasted_iota(jnp.int32, (cl, C_INNER), 0)
    jx = lax.broadcasted_iota(jnp.int32, (cl, C_INNER), 1) & (C_HEAD_DIM - 1)
    inclx = rx >= jx
    strictx = rx > jx
    lane = lax.broadcasted_iota(jnp.int32, (1, LANES), 1)
    first_head = lane < C_HEAD_DIM

    def pad_rows(a):
        if cl == CHUNK:
            return a
        return jnp.concatenate([a, jnp.zeros((CHUNK - cl, a.shape[1]), a.dtype)], axis=0)

    def body(c, carry):
        px, pbc = carry
        r0 = _row_start(c, cl)
        convx, npx = _causal_conv(px, cx_ref[pl.ds(r0, cl), :].astype(F32), cwx_ref, C_CONV)
        convbc, npbc = _causal_conv(pbc, cbc_ref[pl.ds(r0, cl), :].astype(F32), cwbc_ref, C_CONV)
        xs = _silu(convx + cbx_ref[...])
        bcs = _silu(convbc + cbbc_ref[...])
        dtx = _softplus(_dot_pick(sm_ref[pl.ds(r0, cl), :], expand_ref[...]) + biasx)
        adtx = dtx * anegx
        acx = _dot_sel(tril, adtx)
        ex = _dot_sel(tril, jnp.where(strictx, adtx, 0.0))
        decayx = jnp.where(inclx, jnp.exp(ex), 0.0)
        alast = acx[cl - 1:cl, :]
        eac = jnp.exp(acx)
        ealast = jnp.exp(alast)
        xdt = xs * dtx
        xtil = (xdt * jnp.exp(alast - acx)).astype(BF16)
        ys = []
        for g in range(C_GROUPS):
            gs = slice(g * C_GROUP_W, (g + 1) * C_GROUP_W)
            bg = bcs[:, g * C_STATE:(g + 1) * C_STATE].astype(BF16)
            cg = bcs[:, (C_GROUPS + g) * C_STATE:(C_GROUPS + g + 1) * C_STATE].astype(BF16)
            brep = jnp.concatenate([pad_rows(bg)] * (C_GROUP_W // CHUNK), axis=0)
            w = (_dot_nt(cg, brep) * decayx[:, gs]).astype(BF16)
            hg = h_scr[g]
            yoff = _dot(cg, hg.astype(BF16)) * eac[:, gs]
            yd = []
            for pr in range(C_GROUP_W // LANES):
                lo = g * C_GROUP_W + pr * LANES
                xp = pad_rows(xdt[:, lo:lo + LANES])
                bd = jnp.concatenate([jnp.where(first_head, xp, 0.0), jnp.where(first_head, 0.0, xp)], axis=0)
                yd.append(_dot(w[:, pr * LANES:(pr + 1) * LANES], bd.astype(BF16)))
            h_scr[g] = hg * ealast[:, gs] + _dot_tn(bg, xtil[:, gs])
            ys.append(jnp.concatenate(yd, axis=1) + yoff)
        y = jnp.concatenate(ys, axis=1) + cdx * xs
        t = y * _silu(cz_ref[pl.ds(r0, cl), :].astype(F32))
        yc_ref[pl.ds(r0, cl), :] = _rms(t, cn).astype(BF16)
        return npx, npbc

    carry = (cst_ref[0, :, :C_INNER], cst_ref[0, :, C_INNER:])
    carry = body(0, carry) if nc == 1 else lax.fori_loop(0, nc, body, carry)
    cst_o_ref[0, :, :C_INNER] = carry[0][SUBLANES - (C_CONV - 1):, :]
    cst_o_ref[0, :, C_INNER:] = carry[1][SUBLANES - (C_CONV - 1):, :]
    sst_o_ref[0] = h_scr[...]


def _ssd(z, zs, bsz, L, cst, sst, cw, cb, expand, rows):
    T = bsz * L
    cl = min(CHUNK, L)
    nbc = C_XBC - C_INNER
    full = lambda shape: pl.BlockSpec(shape, lambda b: (0,) * len(shape))
    return pl.pallas_call(
        functools.partial(_ssd_kernel, L=L, cl=cl),
        grid=(bsz,),
        in_specs=[
            pl.BlockSpec((L, C_INNER), lambda b: (b, ZB_CX)),
            pl.BlockSpec((L, nbc), lambda b: (b, ZB_CBC)),
            pl.BlockSpec((L, C_INNER), lambda b: (b, ZB_CZ)),
            pl.BlockSpec((L, LANES), lambda b: (b, 0)),
            pl.BlockSpec((1, SUBLANES, C_XBC), lambda b: (b, 0, 0)),
            pl.BlockSpec((1, C_GROUPS, C_STATE, C_GROUP_W), lambda b: (b, 0, 0, 0)),
            full((C_CONV, C_INNER)),
            full((C_CONV, nbc)),
            full((1, C_INNER)),
            full((1, nbc)),
            full((LANES, C_INNER)),
            full((SUBLANES, C_INNER)),
        ],
        out_specs=[
            pl.BlockSpec((L, C_INNER), lambda b: (b, 0)),
            pl.BlockSpec((1, C_CONV - 1, C_XBC), lambda b: (b, 0, 0)),
            pl.BlockSpec((1, C_GROUPS, C_STATE, C_GROUP_W), lambda b: (b, 0, 0, 0)),
        ],
        out_shape=[
            SDS((T, C_INNER), BF16),
            SDS((bsz, C_CONV - 1, C_XBC), F32),
            SDS((bsz, C_GROUPS, C_STATE, C_GROUP_W), F32),
        ],
        scratch_shapes=[pltpu.VMEM((C_GROUPS, C_STATE, C_GROUP_W), F32)],
        compiler_params=_cparams(("parallel",)),
        name="ssd",
    )(z, z, z, zs, cst, sst, cw[:, :C_INNER], cw[:, C_INNER:], cb[:, :C_INNER], cb[:, C_INNER:], expand, rows)


def _merge_kernel(x_ref, ya_ref, yb_ref, yc_ref, ga_ref, gb_ref, gc_ref, wa_ref, wb_ref, wc_ref, wo_ref, o_ref):
    m = jax.nn.sigmoid(ga_ref[...].astype(F32)) * _dot(ya_ref[...], wa_ref[...])
    m = m + jax.nn.sigmoid(gb_ref[...].astype(F32)) * _dot(yb_ref[...], wb_ref[...])
    m = m + jax.nn.sigmoid(gc_ref[...].astype(F32)) * _dot(yc_ref[...], wc_ref[...])
    o_ref[...] = x_ref[...] + _dot(m.astype(BF16), wo_ref[...])


def _merge(x2d, ya, yb, yc, z, wa, wb, wc, wo):
    T = x2d.shape[0]
    tm = min(512, T)
    row = lambda w, cb=0: pl.BlockSpec((tm, w), lambda i: (i, cb))
    full = lambda shape: pl.BlockSpec(shape, lambda i: (0, 0))
    return pl.pallas_call(
        _merge_kernel,
        grid=(T // tm,),
        in_specs=[
            row(D_MODEL), row(A_WIDTH), row(B_WIDTH), row(C_INNER),
            row(D_MODEL, ZB_GATES), row(D_MODEL, ZB_GATES + 1), row(D_MODEL, ZB_GATES + 2),
            full((A_WIDTH, D_MODEL)), full((B_WIDTH, D_MODEL)), full((C_INNER, D_MODEL)), full((D_MODEL, D_MODEL)),
        ],
        out_specs=row(D_MODEL),
        out_shape=SDS((T, D_MODEL), F32),
        compiler_params=_cparams(("parallel",)),
        name="merge",
    )(x2d, ya, yb, yc, z, z, z, wa, wb, wc, wo)


def _memkv_kernel(m_ref, g_ref, wk_ref, wv_ref, k_ref, v_ref):
    hm = _rms(m_ref[0], g_ref[...]).astype(BF16)
    k_ref[0] = _dot(hm, wk_ref[...])
    v_ref[0] = _dot(hm, wv_ref[...])


def _memory_kv(mem, g, wk, wv):
    bsz = mem.shape[0]
    blk = pl.BlockSpec((1, N_MEM, D_MODEL), lambda b: (b, 0, 0))
    full = lambda shape: pl.BlockSpec(shape, lambda b: (0, 0))
    return pl.pallas_call(
        _memkv_kernel,
        grid=(bsz,),
        in_specs=[blk, full((1, D_MODEL)), full((D_MODEL, D_MODEL)), full((D_MODEL, D_MODEL))],
        out_specs=[blk, blk],
        out_shape=[SDS((bsz, N_MEM, D_MODEL), F32)] * 2,
        compiler_params=_cparams(("parallel",)),
        name="memory_kv",
    )(mem, g, wk, wv)


def _xattn_kernel(x_ref, mk_ref, mv_ref, g_ref, wq_ref, wo_ref, o_ref, mk16, mv16):
    @pl.when(pl.program_id(1) == 0)
    def _():
        mk16[...] = mk_ref[0].astype(BF16)
        mv16[...] = mv_ref[0].astype(BF16)

    x = x_ref[...]
    hq = _rms(x, g_ref[...]).astype(BF16)
    q = (_dot(hq, wq_ref[...]) * (X_HEAD_DIM ** -0.5)).astype(BF16)
    outs = []
    for h in range(X_HEADS):
        hs = slice(h * X_HEAD_DIM, (h + 1) * X_HEAD_DIM)
        s = _dot_nt(q[:, hs], mk16[:, hs])
        p = jnp.exp(s - jnp.max(s, axis=-1, keepdims=True))
        l = jnp.sum(p, axis=-1, keepdims=True)
        outs.append((_dot(p.astype(BF16), mv16[:, hs]) / l).astype(BF16))
    o_ref[...] = x + _dot(jnp.concatenate(outs, axis=1), wo_ref[...])


def _xattn(x2d, bsz, L, mk, mv, g, wq, wo):
    tq = min(512, L)
    nq = L // tq
    row = pl.BlockSpec((tq, D_MODEL), lambda b, t: (b * nq + t, 0))
    mem = pl.BlockSpec((1, N_MEM, D_MODEL), lambda b, t: (b, 0, 0))
    full = lambda shape: pl.BlockSpec(shape, lambda b, t: (0, 0))
    return pl.pallas_call(
        _xattn_kernel,
        grid=(bsz, nq),
        in_specs=[row, mem, mem, full((1, D_MODEL)), full((D_MODEL, D_MODEL)), full((D_MODEL, D_MODEL))],
        out_specs=row,
        out_shape=SDS((bsz * L, D_MODEL), F32),
        scratch_shapes=[pltpu.VMEM((N_MEM, D_MODEL), BF16)] * 2,
        compiler_params=_cparams(("parallel", "arbitrary")),
        name="xattn",
    )(x2d, mk, mv, g, wq, wo)


def _ffn_kernel(*refs, tr, nj, final_norm):
    if final_norm:
        x_ref, g_ref, wu_ref, wg_ref, wd_ref, cw_ref, cb_ref, fst_ref, gf_ref = refs[:9]
        o_ref, fst_o_ref, hf_scr, acc_scr, halo_scr = refs[9:]
    else:
        x_ref, g_ref, wu_ref, wg_ref, wd_ref, cw_ref, cb_ref, fst_ref = refs[:8]
        o_ref, fst_o_ref, hf_scr, acc_scr, halo_scr = refs[8:]
    t = pl.program_id(1)
    j = pl.program_id(2)

    @pl.when(j == 0)
    def _():
        hf_scr[...] = _rms(x_ref[...], g_ref[...]).astype(BF16)

    @pl.when(t == 0)
    def _():
        halo_scr[j] = fst_ref[0, 0]

    hf = hf_scr[...]
    u = _dot(hf, wu_ref[...])
    gpre = _dot(hf, wg_ref[...])
    conv, new_halo = _causal_conv(halo_scr[j], gpre, cw_ref, F_CONV)
    halo_scr[j] = new_halo
    fst_o_ref[0, 0, 0] = new_halo
    act = (u * _silu(conv + cb_ref[...])).astype(BF16)
    contrib = _dot(act, wd_ref[...])

    @pl.when(j == 0)
    def _():
        acc_scr[...] = contrib

    @pl.when(j > 0)
    def _():
        acc_scr[...] = acc_scr[...] + contrib

    @pl.when(j == nj - 1)
    def _():
        y = x_ref[...] + acc_scr[...]
        o_ref[...] = _rms(y, gf_ref[...]) if final_norm else y


def _ffn(x2d, bsz, L, g, wup, wdn, cw, cb, fst, gfinal):
    tr = min(512, L)
    nt = L // tr
    tf = D_FF // 2
    nj = D_FF // tf
    final_norm = gfinal is not None
    row = pl.BlockSpec((tr, D_MODEL), lambda b, t, j: (b * nt + t, 0))
    vec = pl.BlockSpec((1, D_MODEL), lambda b, t, j: (0, 0))
    st = pl.BlockSpec((1, 1, SUBLANES, tf), lambda b, t, j: (b, j, 0, 0))
    in_specs = [
        row, vec,
        pl.BlockSpec((D_MODEL, tf), lambda b, t, j: (0, j)),
        pl.BlockSpec((D_MODEL, tf), lambda b, t, j: (0, nj + j)),
        pl.BlockSpec((tf, D_MODEL), lambda b, t, j: (j, 0)),
        pl.BlockSpec((F_CONV, tf), lambda b, t, j: (0, j)),
        pl.BlockSpec((1, tf), lambda b, t, j: (0, j)),
        st,
    ]
    args = [x2d, g, wup, wup, wdn, cw, cb, fst]
    if final_norm:
        in_specs.append(vec)
        args.append(gfinal)
    return pl.pallas_call(
        functools.partial(_ffn_kernel, tr=tr, nj=nj, final_norm=final_norm),
        grid=(bsz, nt, nj),
        in_specs=in_specs,
        out_specs=[row, pl.BlockSpec((1, 1, 1, SUBLANES, tf), lambda b, t, j: (b, t, j, 0, 0))],
        out_shape=[SDS((bsz * L, D_MODEL), F32), SDS((bsz, nt, nj, SUBLANES, tf), F32)],
        scratch_shapes=[
            pltpu.VMEM((tr, D_MODEL), BF16),
            pltpu.VMEM((tr, D_MODEL), F32),
            pltpu.VMEM((nj, SUBLANES, tf), F32),
        ],
        compiler_params=_cparams(("parallel", "arbitrary", "arbitrary")),
        name="ffn",
    )(*args)


def _pad_state(st):
    return jnp.pad(st.astype(F32), ((0, 0), (SUBLANES - st.shape[1], 0), (0, 0)))


def _ffn_state_in(st, nj):
    bsz = st.shape[0]
    p = _pad_state(st).reshape(bsz, SUBLANES, nj, D_FF // nj)
    return jnp.transpose(p, (0, 2, 1, 3))


def _ffn_state_out(st):
    bsz, nj = st.shape[0], st.shape[1]
    p = jnp.transpose(st, (0, 2, 1, 3)).reshape(bsz, SUBLANES, D_FF)
    return p[:, SUBLANES - (F_CONV - 1):, :]


def _head_row(v, width):
    return jnp.repeat(v.astype(F32), width)[None, :]


def _prep_layer(l, p):
    w_in = p['w_in'][l]
    w_main = w_in[:, _PERM_MAIN].astype(BF16)
    w_small = jnp.pad(w_in[:, _PERM_SMALL], ((0, 0), (0, LANES - _PERM_SMALL.size))).astype(BF16)
    prm = jnp.zeros((SUBLANES, LANES), F32)
    prm = prm.at[0, SM_DEC:SM_DEC + B_HEADS].set(p['b_dt_bias'][l])
    prm = prm.at[0, SM_DT:SM_DT + C_HEADS].set(p['c_dt_bias'][l])
    prm = prm.at[1, SM_DEC:SM_DEC + B_HEADS].set(p['b_a_log'][l])
    prm = prm.at[1, SM_DT:SM_DT + C_HEADS].set(p['c_a_log'][l])
    rows = jnp.zeros((SUBLANES, C_INNER), F32)
    rows = rows.at[0].set(_head_row(p['c_dt_bias'][l], C_HEAD_DIM)[0])
    rows = rows.at[1].set(_head_row(p['c_a_log'][l], C_HEAD_DIM)[0])
    rows = rows.at[2].set(_head_row(p['c_d'][l], C_HEAD_DIM)[0])
    rows = rows.at[3].set(p['c_norm'][l].astype(F32))
    return dict(
        norm_mix=p['norm_mix'][l][None, :], w_main=w_main, w_small=w_small,
        rel=p['a_rel_bias'][l],
        b_conv_w=p['b_conv_w'][l], prm=prm, b_norm=p['b_norm'][l][None, :],
        c_conv_w=p['c_conv_w'][l], c_conv_b=p['c_conv_b'][l][None, :], rows=rows,
        wa=p['w_br_a'][l].astype(BF16), wb=p['w_br_b'][l].astype(BF16), wc=p['w_br_c'][l].astype(BF16),
        wo=p['w_out'][l].astype(BF16),
        norm_x=p['norm_x'][l][None, :], norm_mem=p['norm_mem'][l][None, :],
        wxq=p['wx_q'][l].astype(BF16), wxk=p['wx_k'][l].astype(BF16), wxv=p['wx_v'][l].astype(BF16),
        wxo=p['wx_o'][l].astype(BF16),
        norm_ffn=p['norm_ffn'][l][None, :], w_up=p['w_up'][l].astype(BF16), w_down=p['w_down'][l].astype(BF16),
        f_conv_w=p['f_conv_w'][l], f_conv_b=p['f_conv_b'][l][None, :],
    )


def _expand_matrix():
    e = np.zeros((LANES, C_INNER), np.float32)
    for h in range(C_HEADS):
        e[SM_DT + h, h * C_HEAD_DIM:(h + 1) * C_HEAD_DIM] = 1.0
    return jnp.asarray(e)


def _layer(x2d, bsz, L, lw, a_past, b_conv, b_rec, c_conv, c_ssm, f_conv, mk, mv, gfinal, expand):
    z, zs = _in_proj(x2d, lw['norm_mix'], lw['w_main'], lw['w_small'])

    C = min(CHUNK, L)
    G = 2 if L // C >= 2 else 1
    bias = _attn_bias(lw['rel'], C, G)
    past = None
    if a_past is not None:
        past = tuple(t.reshape(bsz, A_WINDOW, A_WIDTH) for t in a_past)
    ya = _band_attn(z, bsz, L, C, G, bias, past)
    keep = min(A_WINDOW, L)
    z3 = z.reshape(bsz, L, NZ)
    a_k = z3[:, L - keep:, A_WIDTH:2 * A_WIDTH].astype(F32).reshape(bsz, keep, A_HEADS, A_HEAD_DIM)
    a_v = z3[:, L - keep:, 2 * A_WIDTH:3 * A_WIDTH].astype(F32).reshape(bsz, keep, A_HEADS, A_HEAD_DIM)

    yb, b_conv_new, b_rec_new = _gdn(z, zs, bsz, L, _pad_state(b_conv), b_rec.astype(F32),
                                     lw['b_conv_w'], lw['prm'], lw['b_norm'])

    sst = jnp.transpose(c_ssm.astype(F32).reshape(bsz, C_GROUPS, C_HEADS // C_GROUPS, C_HEAD_DIM, C_STATE),
                        (0, 1, 4, 2, 3)).reshape(bsz, C_GROUPS, C_STATE, C_GROUP_W)
    yc, c_conv_new, sst_new = _ssd(z, zs, bsz, L, _pad_state(c_conv), sst, lw['c_conv_w'], lw['c_conv_b'],
                                   expand, lw['rows'])
    c_ssm_new = jnp.transpose(sst_new.reshape(bsz, C_GROUPS, C_STATE, C_HEADS // C_GROUPS, C_HEAD_DIM),
                              (0, 1, 3, 4, 2)).reshape(bsz, C_HEADS, C_HEAD_DIM, C_STATE)

    x2d = _merge(x2d, ya, yb, yc, z, lw['wa'], lw['wb'], lw['wc'], lw['wo'])
    x2d = _xattn(x2d, bsz, L, mk, mv, lw['norm_x'], lw['wxq'], lw['wxo'])

    nj = 2
    x2d, fst_new = _ffn(x2d, bsz, L, lw['norm_ffn'], lw['w_up'], lw['w_down'], lw['f_conv_w'], lw['f_conv_b'],
                        _ffn_state_in(f_conv, nj), gfinal)
    return x2d, (a_k, a_v, b_conv_new, b_rec_new, c_conv_new, c_ssm_new, _ffn_state_out(fst_new[:, -1]))


def kernel(x_prompt, x_sample, cache_attn_k, cache_attn_v, state_b_conv, state_b_rec, state_c_conv, state_c_ssm, state_ffn_conv, cache_mem_k, cache_mem_v, mem_prompt, norm_mix, w_in, a_rel_bias, b_conv_w, b_a_log, b_dt_bias, b_norm, c_conv_w, c_conv_b, c_dt_bias, c_a_log, c_d, c_norm, w_br_a, w_br_b, w_br_c, w_out, norm_x, norm_mem, wx_q, wx_k, wx_v, wx_o, norm_ffn, w_up, f_conv_w, f_conv_b, w_down, norm_final):
    params = dict(norm_mix=norm_mix, w_in=w_in, a_rel_bias=a_rel_bias, b_conv_w=b_conv_w, b_a_log=b_a_log,
                  b_dt_bias=b_dt_bias, b_norm=b_norm, c_conv_w=c_conv_w, c_conv_b=c_conv_b, c_dt_bias=c_dt_bias,
                  c_a_log=c_a_log, c_d=c_d, c_norm=c_norm, w_br_a=w_br_a, w_br_b=w_br_b, w_br_c=w_br_c,
                  w_out=w_out, norm_x=norm_x, norm_mem=norm_mem, wx_q=wx_q, wx_k=wx_k, wx_v=wx_v, wx_o=wx_o,
                  norm_ffn=norm_ffn, w_up=w_up, f_conv_w=f_conv_w, f_conv_b=f_conv_b, w_down=w_down)
    nb, seq, _ = x_prompt.shape
    db, dseq, _ = x_sample.shape
    expand = _expand_matrix()
    gfin = norm_final[None, :]
    xp = x_prompt.reshape(nb * seq, D_MODEL)
    xs = x_sample.reshape(db * dseq, D_MODEL)
    p_states, s_states, p_mk, p_mv = [], [], [], []
    for l in range(DEPTH):
        lw = _prep_layer(l, params)
        last = gfin if l == DEPTH - 1 else None
        mk, mv = _memory_kv(mem_prompt, lw['norm_mem'], lw['wxk'], lw['wxv'])
        xp, st_p = _layer(
            xp, nb, seq, lw, None,
            jnp.zeros((nb, B_CONV - 1, 3 * B_WIDTH), F32),
            jnp.zeros((nb, B_HEADS, B_HEAD_DIM, B_HEAD_DIM), F32),
            jnp.zeros((nb, C_CONV - 1, C_XBC), F32),
            jnp.zeros((nb, C_HEADS, C_HEAD_DIM, C_STATE), F32),
            jnp.zeros((nb, F_CONV - 1, D_FF), F32),
            mk, mv, last, expand)
        p_states.append(st_p)
        p_mk.append(mk.reshape(nb, N_MEM, X_HEADS, X_HEAD_DIM))
        p_mv.append(mv.reshape(nb, N_MEM, X_HEADS, X_HEAD_DIM))
        xs, st_s = _layer(
            xs, db, dseq, lw, (cache_attn_k[l], cache_attn_v[l]),
            state_b_conv[l], state_b_rec[l], state_c_conv[l], state_c_ssm[l], state_ffn_conv[l],
            cache_mem_k[l].reshape(db, N_MEM, D_MODEL), cache_mem_v[l].reshape(db, N_MEM, D_MODEL),
            last, expand)
        s_states.append(st_s)

    y_prompt = xp.reshape(nb, seq, D_MODEL)
    y_sample = xs.reshape(db, dseq, D_MODEL)
    pst = [jnp.stack([s[i] for s in p_states]) for i in range(7)]
    sst = [jnp.stack([s[i] for s in s_states]) for i in range(7)]
    return (y_prompt, y_sample, pst[0], pst[1], pst[2], pst[3], pst[4], pst[5], pst[6],
            jnp.stack(p_mk), jnp.stack(p_mv),
            sst[0], sst[1], sst[2], sst[3], sst[4], sst[5], sst[6])
```

```python
import functools

import jax
import jax.numpy as jnp
import numpy as np
from jax import lax
from jax.experimental import pallas as pl
from jax.experimental.pallas import tpu as pltpu

F32 = jnp.float32
BF16 = jnp.bfloat16
SDS = jax.ShapeDtypeStruct

D_MODEL = 1024
DEPTH = 2
CHUNK = 64
N_MEM = 256
EPS = 1e-6

A_HEADS = 8
A_HEAD_DIM = 64
A_WIDTH = A_HEADS * A_HEAD_DIM
A_WINDOW = 8 * CHUNK
A_MAX_REL = 128

B_HEADS = 4
B_HEAD_DIM = 128
B_WIDTH = B_HEADS * B_HEAD_DIM
B_CONV = 4

C_HEADS = 16
C_HEAD_DIM = 64
C_INNER = C_HEADS * C_HEAD_DIM
C_GROUPS = 2
C_STATE = 128
C_XBC = C_INNER + 2 * C_GROUPS * C_STATE
C_CONV = 4
C_GROUP_W = C_INNER // C_GROUPS

X_HEADS = 4
X_HEAD_DIM = D_MODEL // X_HEADS

D_FF = 2816
F_CONV = 3

LANES = 128
SUBLANES = 8
NEG = -1e30
FFN_ROWS_PER_STEP = 1024
FFN_CHUNK = 256
GDN_SEQS_PER_STEP = 4
GDN_ROWS_PER_STEP = 512
ATTN_HEADS_PER_STEP = 4
VMEM_LIMIT = 56 * 1024 * 1024

NZ = 9216
ZW_A = A_WIDTH
ZB_BQKV = 1
ZB_CZ = 3
ZB_CX = 4
ZB_CBC = 10
ZB_BGATE = 11
ZB_GATES = 6
SM_BETA = 0
SM_DEC = 4
SM_DT = 8

_O_BQKV = 3 * A_WIDTH
_O_BETA = _O_BQKV + 3 * B_WIDTH
_O_DEC = _O_BETA + B_HEADS
_O_BGATE = _O_DEC + B_HEADS
_O_CZ = _O_BGATE + B_WIDTH
_O_CXBC = _O_CZ + C_INNER
_O_CDT = _O_CXBC + C_XBC
_O_GATES = _O_CDT + C_HEADS
_SEGS_MAIN = ((0, _O_BETA), (_O_CZ, _O_CXBC), (_O_CXBC, _O_CDT), (_O_BGATE, _O_CZ), (_O_GATES, _O_GATES + 3 * D_MODEL))
_SEGS_SMALL = ((_O_BETA, _O_BGATE), (_O_CDT, _O_GATES))


def _cparams(sem):
    return pltpu.CompilerParams(dimension_semantics=sem, vmem_limit_bytes=VMEM_LIMIT)


def _rms(x, g):
    return x * lax.rsqrt(jnp.mean(x * x, axis=-1, keepdims=True) + EPS) * g


def _silu(x):
    return x * jax.nn.sigmoid(x)


def _softplus(x):
    return jnp.maximum(x, 0.0) + jnp.log1p(jnp.exp(-jnp.abs(x)))


def _dot(a, b):
    return jnp.dot(a, b, preferred_element_type=F32)


def _pieces(a, n):
    out = []
    for _ in range(n - 1):
        p = a.astype(BF16)
        out.append(p)
        a = a - p.astype(F32)
    out.append(a.astype(BF16))
    return out


def _dot_sel(sel, b):
    s16 = sel.astype(BF16)
    b1, b2, b3 = _pieces(b, 3)
    return _dot(s16, b1) + (_dot(s16, b2) + _dot(s16, b3))


def _dot_pick(a, sel):
    s16 = sel.astype(BF16)
    a1, a2, a3 = _pieces(a, 3)
    return _dot(a1, s16) + (_dot(a2, s16) + _dot(a3, s16))


def _dot_x3(a, b):
    ah, al = _pieces(a, 2)
    bh, bl = _pieces(b, 2)
    return _dot(ah, bh) + (_dot(ah, bl) + _dot(al, bh))


def _dot_nt(a, b):
    return lax.dot_general(a, b, (((1,), (1,)), ((), ())), preferred_element_type=F32)


def _dot_tn(a, b):
    return lax.dot_general(a, b, (((0,), (0,)), ((), ())), preferred_element_type=F32)


def _row_start(i, n):
    return i * n if isinstance(i, int) else pl.multiple_of(i * n, n)


def _causal_conv(prev8, cur, w_ref, taps):
    ext = jnp.concatenate([prev8, cur], axis=0)
    acc = cur * w_ref[taps - 1:taps, :]
    for s in range(1, taps):
        acc = acc + pltpu.roll(ext, s, 0)[SUBLANES:, :] * w_ref[taps - 1 - s:taps - s, :]
    return acc, ext[cur.shape[0]:, :]


def _inproj_kernel(x_ref, g_ref, w_ref, ws_ref, z_ref, zs_ref, h_scr):
    @pl.when(pl.program_id(1) == 0)
    def _():
        hb = _rms(x_ref[...], g_ref[...]).astype(BF16)
        h_scr[...] = hb
        zs_ref[...] = _dot(hb, ws_ref[...])

    z_ref[...] = _dot(h_scr[...], w_ref[...]).astype(BF16)


def _in_proj(x2d, g, w_main, w_small):
    T = x2d.shape[0]
    tm = min(1024, T)
    tn = 1536
    return pl.pallas_call(
        _inproj_kernel,
        grid=(T // tm, NZ // tn),
        in_specs=[
            pl.BlockSpec((tm, D_MODEL), lambda i, j: (i, 0)),
            pl.BlockSpec((1, D_MODEL), lambda i, j: (0, 0)),
            pl.BlockSpec((D_MODEL, tn), lambda i, j: (0, j)),
            pl.BlockSpec((D_MODEL, LANES), lambda i, j: (0, 0)),
        ],
        out_specs=[
            pl.BlockSpec((tm, tn), lambda i, j: (i, j)),
            pl.BlockSpec((tm, LANES), lambda i, j: (i, 0)),
        ],
        out_shape=[SDS((T, NZ), BF16), SDS((T, LANES), F32)],
        scratch_shapes=[pltpu.VMEM((tm, D_MODEL), BF16)],
        compiler_params=_cparams(("parallel", "arbitrary")),
        name="in_proj",
    )(x2d, g, w_main, w_small)


def _attn_kernel(*refs, L, C, G, has_past):
    if has_past:
        q_ref, k_ref, v_ref, pk_ref, pv_ref, bias_ref, o_ref, kx, vx = refs
        kx[0:A_WINDOW, :] = pk_ref[0].astype(BF16)
        vx[0:A_WINDOW, :] = pv_ref[0].astype(BF16)
    else:
        q_ref, k_ref, v_ref, bias_ref, o_ref, kx, vx = refs
        kx[0:A_WINDOW, :] = jnp.zeros((A_WINDOW, A_WIDTH), BF16)
        vx[0:A_WINDOW, :] = jnp.zeros((A_WINDOW, A_WIDTH), BF16)
    kx[A_WINDOW:A_WINDOW + L, :] = k_ref[...]
    vx[A_WINDOW:A_WINDOW + L, :] = v_ref[...]

    GC = G * C
    NB = A_WINDOW + GC
    lane = lax.broadcasted_iota(jnp.int32, (1, LANES), 1)
    first_head = lane < A_HEAD_DIM
    col = lax.broadcasted_iota(jnp.int32, (1, NB), 1)
    scale = A_HEAD_DIM ** -0.5

    def group(g, carry, masked):
        r0 = _row_start(g, GC)
        q = q_ref[pl.ds(r0, GC), :] * jnp.asarray(scale, BF16)
        kb = kx[pl.ds(r0, NB), :]
        vb = vx[pl.ds(r0, NB), :]
        for h0 in range(0, A_HEADS, ATTN_HEADS_PER_STEP):
            hs = range(h0, h0 + ATTN_HEADS_PER_STEP)
            sl = {h: slice((h // 2) * LANES, (h // 2 + 1) * LANES) for h in hs}
            msk = {h: first_head if h % 2 == 0 else jnp.logical_not(first_head) for h in hs}
            s = {h: _dot_nt(jnp.where(msk[h], q[:, sl[h]], jnp.zeros((GC, LANES), BF16)), kb[:, sl[h]]) for h in hs}
            s = {h: s[h] + bias_ref[h] for h in hs}
            if masked:
                s = {h: jnp.where(r0 + col >= A_WINDOW, s[h], NEG) for h in hs}
            p = {h: jnp.exp(s[h] - jnp.max(s[h], axis=-1, keepdims=True)) for h in hs}
            l = {h: jnp.sum(p[h], axis=-1, keepdims=True) for h in hs}
            pv = {h: _dot(p[h].astype(BF16), vb[:, sl[h]]) / l[h] for h in hs}
            for h in hs:
                if h % 2 == 1:
                    o_ref[pl.ds(r0, GC), sl[h]] = jnp.where(first_head, pv[h - 1], pv[h]).astype(BF16)
        return carry

    ng = L // GC
    n_masked = 0 if has_past else min(ng, -(-A_WINDOW // GC))
    if ng == 1:
        group(0, 0, n_masked > 0)
    else:
        if n_masked:
            lax.fori_loop(0, n_masked, functools.partial(group, masked=True), 0)
        if ng > n_masked:
            lax.fori_loop(n_masked, ng, functools.partial(group, masked=False), 0)


def _attn_bias(table, C, G):
    GC = G * C
    NB = A_WINDOW + GC
    r = np.arange(GC)[:, None]
    j = np.arange(NB)[None, :]
    rel = np.clip(r + A_WINDOW - j, -A_MAX_REL, A_MAX_REL) + A_MAX_REL
    lo = (r // C) * C
    allowed = (j >= lo) & (j < lo + A_WINDOW + C)
    b = jnp.transpose(table[rel], (2, 0, 1)).astype(F32)
    return jnp.where(allowed[None], b, NEG)


def _band_attn(z, bsz, L, C, G, bias, past):
    T = bsz * L
    has_past = past is not None
    NB = A_WINDOW + G * C
    in_specs = [pl.BlockSpec((L, A_WIDTH), lambda b, i=i: (b, i)) for i in range(3)]
    args = [z, z, z]
    if has_past:
        in_specs += [pl.BlockSpec((1, A_WINDOW, A_WIDTH), lambda b: (b, 0, 0))] * 2
        args += list(past)
    in_specs.append(pl.BlockSpec((A_HEADS, G * C, NB), lambda b: (0, 0, 0)))
    args.append(bias)
    return pl.pallas_call(
        functools.partial(_attn_kernel, L=L, C=C, G=G, has_past=has_past),
        grid=(bsz,),
        in_specs=in_specs,
        out_specs=pl.BlockSpec((L, A_WIDTH), lambda b: (b, 0)),
        out_shape=SDS((T, A_WIDTH), BF16),
        scratch_shapes=[pltpu.VMEM((A_WINDOW + L, A_WIDTH), BF16)] * 2,
        compiler_params=_cparams(("parallel",)),
        name="band_attn",
    )(*args)


def _gdn_kernel(qkv_ref, sm_ref, gate_ref, cst_ref, rst_ref, cw_ref, prm_ref, bn_ref,
                yb_ref, cst_o_ref, rst_o_ref, s_scr, prev_scr, *, nb, lb, cl):
    t = pl.program_id(1)
    nc = lb // cl

    @pl.when(t == 0)
    def _():
        for n in range(nb):
            for h in range(B_HEADS):
                s_scr[n * B_HEADS + h] = rst_ref[n, h]
        prev_scr[...] = cst_ref[...]

    bias_row = prm_ref[0:1, :]
    aneg_row = -jnp.exp(prm_ref[1:2, :])
    bn = bn_ref[...]
    ri = lax.broadcasted_iota(jnp.int32, (cl, cl), 0)
    ci = lax.broadcasted_iota(jnp.int32, (cl, cl), 1)
    incl = ri >= ci
    strict = ri > ci
    tril = incl.astype(F32)
    eye = (ri == ci).astype(F32)
    sh = int(np.log2(SUBLANES))
    diag_blk = (ri >> sh) == (ci >> sh)
    sub_blk = []
    while (1 << sh) < cl:
        sub_blk.append(((ri >> (sh + 1)) == (ci >> (sh + 1))) & (((ri >> sh) & 1) == 1) & (((ci >> sh) & 1) == 0))
        sh += 1
    U = [(n, h) for n in range(nb) for h in range(B_HEADS)]

    def body(c, carry):
        r0 = _row_start(c, cl)
        rows = pl.ds(r0, cl)
        act, beta_all, gv = [], [], []
        for n in range(nb):
            conv, new_prev = _causal_conv(prev_scr[n], qkv_ref[n, rows, :].astype(F32), cw_ref, B_CONV)
            prev_scr[n] = new_prev
            act.append(_silu(conv))
            smc = sm_ref[n, rows, :]
            beta_all.append(jax.nn.sigmoid(smc))
            gv.append(_softplus(smc + bias_row) * aneg_row)
        gcs = [_dot_sel(tril, gv[n]) for n in range(nb)]
        q = {(n, h): act[n][:, h * B_HEAD_DIM:(h + 1) * B_HEAD_DIM] for n, h in U}
        k = {(n, h): act[n][:, B_WIDTH + h * B_HEAD_DIM:B_WIDTH + (h + 1) * B_HEAD_DIM] for n, h in U}
        v = {(n, h): act[n][:, 2 * B_WIDTH + h * B_HEAD_DIM:2 * B_WIDTH + (h + 1) * B_HEAD_DIM] for n, h in U}
        q = {u: q[u] * lax.rsqrt(jnp.sum(q[u] * q[u], axis=-1, keepdims=True) + EPS) * (B_HEAD_DIM ** -0.5) for u in U}
        k = {u: k[u] * lax.rsqrt(jnp.sum(k[u] * k[u], axis=-1, keepdims=True) + EPS) for u in U}
        beta = {(n, h): beta_all[n][:, SM_BETA + h:SM_BETA + h + 1] for n, h in U}
        g = {(n, h): gv[n][:, SM_DEC + h:SM_DEC + h + 1] for n, h in U}
        gc = {(n, h): gcs[n][:, SM_DEC + h:SM_DEC + h + 1] for n, h in U}
        gl = {(n, h): gcs[n][cl - 1:cl, SM_DEC + h:SM_DEC + h + 1] for n, h in U}
        e = {u: _dot_sel(tril, jnp.where(strict, g[u], 0.0)) for u in U}
        decay = {u: jnp.where(incl, jnp.exp(e[u]), 0.0) for u in U}
        kb = {u: k[u] * beta[u] for u in U}
        k16 = {u: k[u].astype(BF16) for u in U}
        a = {u: _dot_nt(kb[u].astype(BF16), k16[u]) for u in U}
        lm = {u: jnp.where(strict, a[u] * decay[u], 0.0) for u in U}
        m = {u: jnp.where(diag_blk, -lm[u], 0.0) for u in U}
        p = {u: eye + m[u] for u in U}
        for _ in range(2):
            m = {u: _dot_x3(m[u], m[u]) for u in U}
            p = {u: p[u] + _dot_x3(p[u], m[u]) for u in U}
        for cmask in sub_blk:
            cx = {u: _dot_x3(jnp.where(cmask, lm[u], 0.0), p[u]) for u in U}
            p = {u: p[u] - _dot_x3(p[u], cx[u]) for u in U}
        egc = {u: jnp.exp(gc[u]) for u in U}
        sol = {u: _dot_x3(p[u], jnp.concatenate([v[u] * beta[u], kb[u] * egc[u]], axis=1)) for u in U}
        qk = {u: _dot_nt(q[u].astype(BF16), k16[u]) * decay[u] for u in U}
        s0 = {(n, h): s_scr[n * B_HEADS + h] for n, h in U}
        s16 = {u: s0[u].astype(BF16) for u in U}
        uu = {u: sol[u][:, :B_HEAD_DIM] - _dot(sol[u][:, B_HEAD_DIM:].astype(BF16), s16[u]) for u in U}
        u16 = {u: uu[u].astype(BF16) for u in U}
        kt = {u: (k[u] * jnp.exp(gl[u] - gc[u])).astype(BF16) for u in U}
        snew = {u: s0[u] * jnp.exp(gl[u]) + _dot_tn(kt[u], u16[u]) for u in U}
        o = {u: _dot((q[u] * egc[u]).astype(BF16), s16[u]) + _dot(qk[u].astype(BF16), u16[u]) for u in U}
        for n, h in U:
            hs = slice(h * B_HEAD_DIM, (h + 1) * B_HEAD_DIM)
            s_scr[n * B_HEADS + h] = snew[n, h]
            gate = gate_ref[n, rows, hs].astype(F32)
            yb_ref[n, rows, hs] = (_rms(o[n, h], bn) * _silu(gate)).astype(BF16)
        return carry

    if nc == 1:
        body(0, 0)
    else:
        lax.fori_loop(0, nc, body, 0)

    @pl.when(t == pl.num_programs(1) - 1)
    def _():
        cst_o_ref[...] = prev_scr[:, SUBLANES - (B_CONV - 1):, :]
        for n in range(nb):
            for h in range(B_HEADS):
                rst_o_ref[n, h] = s_scr[n * B_HEADS + h]


def _gdn(z, zs, bsz, L, cst, rst, cw, prm, bn):
    cl = min(CHUNK, L)
    nb = GDN_SEQS_PER_STEP
    lb = min(GDN_ROWS_PER_STEP, L)
    assert bsz % nb == 0 and L % lb == 0 and lb % cl == 0
    z3 = z.reshape(bsz, L, NZ)
    zs3 = zs.reshape(bsz, L, LANES)
    yb, cst_o, rst_o = pl.pallas_call(
        functools.partial(_gdn_kernel, nb=nb, lb=lb, cl=cl),
        grid=(bsz // nb, L // lb),
        in_specs=[
            pl.BlockSpec((nb, lb, 3 * B_WIDTH), lambda b, t: (b, t, ZB_BQKV)),
            pl.BlockSpec((nb, lb, LANES), lambda b, t: (b, t, 0)),
            pl.BlockSpec((nb, lb, B_WIDTH), lambda b, t: (b, t, ZB_BGATE)),
            pl.BlockSpec((nb, SUBLANES, 3 * B_WIDTH), lambda b, t: (b, 0, 0)),
            pl.BlockSpec((nb, B_HEADS, B_HEAD_DIM, B_HEAD_DIM), lambda b, t: (b, 0, 0, 0)),
            pl.BlockSpec((B_CONV, 3 * B_WIDTH), lambda b, t: (0, 0)),
            pl.BlockSpec((SUBLANES, LANES), lambda b, t: (0, 0)),
            pl.BlockSpec((1, B_HEAD_DIM), lambda b, t: (0, 0)),
        ],
        out_specs=[
            pl.BlockSpec((nb, lb, B_WIDTH), lambda b, t: (b, t, 0)),
            pl.BlockSpec((nb, B_CONV - 1, 3 * B_WIDTH), lambda b, t: (b, 0, 0)),
            pl.BlockSpec((nb, B_HEADS, B_HEAD_DIM, B_HEAD_DIM), lambda b, t: (b, 0, 0, 0)),
        ],
        out_shape=[
            SDS((bsz, L, B_WIDTH), BF16),
            SDS((bsz, B_CONV - 1, 3 * B_WIDTH), F32),
            SDS((bsz, B_HEADS, B_HEAD_DIM, B_HEAD_DIM), F32),
        ],
        scratch_shapes=[
            pltpu.VMEM((nb * B_HEADS, B_HEAD_DIM, B_HEAD_DIM), F32),
            pltpu.VMEM((nb, SUBLANES, 3 * B_WIDTH), F32),
        ],
        compiler_params=_cparams(("parallel", "arbitrary")),
        name="gdn",
    )(z3, zs3, z3, cst, rst, cw, prm, bn)
    return yb.reshape(bsz * L, B_WIDTH), cst_o, rst_o


def _ssd_kernel(cx_ref, cbc_ref, cz_ref, sm_ref, cst_ref, sst_ref, cwx_ref, cwbc_ref, cbx_ref, cbbc_ref,
                expand_ref, rows_ref, yc_ref, cst_o_ref, sst_o_ref, h_scr, *, L, cl):
    nc = L // cl
    h_scr[...] = sst_ref[0]
    biasx = rows_ref[0:1, :]
    anegx = -jnp.exp(rows_ref[1:2, :])
    cdx = rows_ref[2:3, :]
    cn = rows_ref[3:4, :]
    ri = lax.broadcasted_iota(jnp.int32, (cl, cl), 0)
    ci = lax.broadcasted_iota(jnp.int32, (cl, cl), 1)
    tril = (ri >= ci).astype(F32)
    rx = lax.broadcasted_iota(jnp.int32, (cl, C_INNER), 0)
    jx = lax.broadcasted_iota(jnp.int32, (cl, C_INNER), 1) & (C_HEAD_DIM - 1)
    inclx = rx >= jx
    strictx = rx > jx
    lane = lax.broadcasted_iota(jnp.int32, (1, LANES), 1)
    first_head = lane < C_HEAD_DIM

    def pad_rows(a):
        if cl == CHUNK:
            return a
        return jnp.concatenate([a, jnp.zeros((CHUNK - cl, a.shape[1]), a.dtype)], axis=0)

    def body(c, carry):
        px, pbc = carry
        r0 = _row_start(c, cl)
        convx, npx = _causal_conv(px, cx_ref[pl.ds(r0, cl), :].astype(F32), cwx_ref, C_CONV)
        convbc, npbc = _causal_conv(pbc, cbc_ref[pl.ds(r0, cl), :].astype(F32), cwbc_ref, C_CONV)
        xs = _silu(convx + cbx_ref[...])
        bcs = _silu(convbc + cbbc_ref[...])
        dtx = _softplus(_dot_pick(sm_ref[pl.ds(r0, cl), :], expand_ref[...]) + biasx)
        adtx = dtx * anegx
        acx = _dot_sel(tril, adtx)
        ex = _dot_sel(tril, jnp.where(strictx, adtx, 0.0))
        decayx = jnp.where(inclx, jnp.exp(ex), 0.0)
        alast = acx[cl - 1:cl, :]
        eac = jnp.exp(acx)
        ealast = jnp.exp(alast)
        xdt = xs * dtx
        xtil = (xdt * jnp.exp(alast - acx)).astype(BF16)
        ys = []
        for g in range(C_GROUPS):
            gs = slice(g * C_GROUP_W, (g + 1) * C_GROUP_W)
            bg = bcs[:, g * C_STATE:(g + 1) * C_STATE].astype(BF16)
            cg = bcs[:, (C_GROUPS + g) * C_STATE:(C_GROUPS + g + 1) * C_STATE].astype(BF16)
            brep = jnp.concatenate([pad_rows(bg)] * (C_GROUP_W // CHUNK), axis=0)
            w = (_dot_nt(cg, brep) * decayx[:, gs]).astype(BF16)
            hg = h_scr[g]
            yoff = _dot(cg, hg.astype(BF16)) * eac[:, gs]
            yd = []
            for pr in range(C_GROUP_W // LANES):
                lo = g * C_GROUP_W + pr * LANES
                xp = pad_rows(xdt[:, lo:lo + LANES])
                bd = jnp.concatenate([jnp.where(first_head, xp, 0.0), jnp.where(first_head, 0.0, xp)], axis=0)
                yd.append(_dot(w[:, pr * LANES:(pr + 1) * LANES], bd.astype(BF16)))
            h_scr[g] = hg * ealast[:, gs] + _dot_tn(bg, xtil[:, gs])
            ys.append(jnp.concatenate(yd, axis=1) + yoff)
        y = jnp.concatenate(ys, axis=1) + cdx * xs
        t = y * _silu(cz_ref[pl.ds(r0, cl), :].astype(F32))
        yc_ref[pl.ds(r0, cl), :] = _rms(t, cn).astype(BF16)
        return npx, npbc

    carry = (cst_ref[0, :, :C_INNER], cst_ref[0, :, C_INNER:])
    carry = body(0, carry) if nc == 1 else lax.fori_loop(0, nc, body, carry)
    cst_o_ref[0, :, :C_INNER] = carry[0][SUBLANES - (C_CONV - 1):, :]
    cst_o_ref[0, :, C_INNER:] = carry[1][SUBLANES - (C_CONV - 1):, :]
    sst_o_ref[0] = h_scr[...]


def _ssd(z, zs, bsz, L, cst, sst, cw, cb, expand, rows):
    T = bsz * L
    cl = min(CHUNK, L)
    nbc = C_XBC - C_INNER
    full = lambda shape: pl.BlockSpec(shape, lambda b: (0,) * len(shape))
    return pl.pallas_call(
        functools.partial(_ssd_kernel, L=L, cl=cl),
        grid=(bsz,),
        in_specs=[
            pl.BlockSpec((L, C_INNER), lambda b: (b, ZB_CX)),
            pl.BlockSpec((L, nbc), lambda b: (b, ZB_CBC)),
            pl.BlockSpec((L, C_INNER), lambda b: (b, ZB_CZ)),
            pl.BlockSpec((L, LANES), lambda b: (b, 0)),
            pl.BlockSpec((1, SUBLANES, C_XBC), lambda b: (b, 0, 0)),
            pl.BlockSpec((1, C_GROUPS, C_STATE, C_GROUP_W), lambda b: (b, 0, 0, 0)),
            full((C_CONV, C_INNER)),
            full((C_CONV, nbc)),
            full((1, C_INNER)),
            full((1, nbc)),
            full((LANES, C_INNER)),
            full((SUBLANES, C_INNER)),
        ],
        out_specs=[
            pl.BlockSpec((L, C_INNER), lambda b: (b, 0)),
            pl.BlockSpec((1, C_CONV - 1, C_XBC), lambda b: (b, 0, 0)),
            pl.BlockSpec((1, C_GROUPS, C_STATE, C_GROUP_W), lambda b: (b, 0, 0, 0)),
        ],
        out_shape=[
            SDS((T, C_INNER), BF16),
            SDS((bsz, C_CONV - 1, C_XBC), F32),
            SDS((bsz, C_GROUPS, C_STATE, C_GROUP_W), F32),
        ],
        scratch_shapes=[pltpu.VMEM((C_GROUPS, C_STATE, C_GROUP_W), F32)],
        compiler_params=_cparams(("parallel",)),
        name="ssd",
    )(z, z, z, zs, cst, sst, cw[:, :C_INNER], cw[:, C_INNER:], cb[:, :C_INNER], cb[:, C_INNER:], expand, rows)


def _merge_kernel(x_ref, ya_ref, yb_ref, yc_ref, ga_ref, gb_ref, gc_ref, wa_ref, wb_ref, wc_ref, wo_ref, o_ref):
    m = jax.nn.sigmoid(ga_ref[...].astype(F32)) * _dot(ya_ref[...], wa_ref[...])
    m = m + jax.nn.sigmoid(gb_ref[...].astype(F32)) * _dot(yb_ref[...], wb_ref[...])
    m = m + jax.nn.sigmoid(gc_ref[...].astype(F32)) * _dot(yc_ref[...], wc_ref[...])
    o_ref[...] = x_ref[...] + _dot(m.astype(BF16), wo_ref[...])


def _merge(x2d, ya, yb, yc, z, wa, wb, wc, wo):
    T = x2d.shape[0]
    tm = min(512, T)
    row = lambda w, cb=0: pl.BlockSpec((tm, w), lambda i: (i, cb))
    full = lambda shape: pl.BlockSpec(shape, lambda i: (0, 0))
    return pl.pallas_call(
        _merge_kernel,
        grid=(T // tm,),
        in_specs=[
            row(D_MODEL), row(A_WIDTH), row(B_WIDTH), row(C_INNER),
            row(D_MODEL, ZB_GATES), row(D_MODEL, ZB_GATES + 1), row(D_MODEL, ZB_GATES + 2),
            full((A_WIDTH, D_MODEL)), full((B_WIDTH, D_MODEL)), full((C_INNER, D_MODEL)), full((D_MODEL, D_MODEL)),
        ],
        out_specs=row(D_MODEL),
        out_shape=SDS((T, D_MODEL), F32),
        compiler_params=_cparams(("parallel",)),
        name="merge",
    )(x2d, ya, yb, yc, z, z, z, wa, wb, wc, wo)


def _memkv_kernel(m_ref, g_ref, wk_ref, wv_ref, k_ref, v_ref):
    hm = _rms(m_ref[0], g_ref[...]).astype(BF16)
    k_ref[0] = _dot(hm, wk_ref[...])
    v_ref[0] = _dot(hm, wv_ref[...])


def _memory_kv(mem, g, wk, wv):
    bsz = mem.shape[0]
    blk = pl.BlockSpec((1, N_MEM, D_MODEL), lambda b: (b, 0, 0))
    full = lambda shape: pl.BlockSpec(shape, lambda b: (0, 0))
    return pl.pallas_call(
        _memkv_kernel,
        grid=(bsz,),
        in_specs=[blk, full((1, D_MODEL)), full((D_MODEL, D_MODEL)), full((D_MODEL, D_MODEL))],
        out_specs=[blk, blk],
        out_shape=[SDS((bsz, N_MEM, D_MODEL), F32)] * 2,
        compiler_params=_cparams(("parallel",)),
        name="memory_kv",
    )(mem, g, wk, wv)


def _xattn_kernel(x_ref, mk_ref, mv_ref, g_ref, wq_ref, wo_ref, o_ref, mk16, mv16):
    @pl.when(pl.program_id(1) == 0)
    def _():
        mk16[...] = mk_ref[0].astype(BF16)
        mv16[...] = mv_ref[0].astype(BF16)

    x = x_ref[...]
    hq = _rms(x, g_ref[...]).astype(BF16)
    q = (_dot(hq, wq_ref[...]) * (X_HEAD_DIM ** -0.5)).astype(BF16)
    outs = []
    for h in range(X_HEADS):
        hs = slice(h * X_HEAD_DIM, (h + 1) * X_HEAD_DIM)
        s = _dot_nt(q[:, hs], mk16[:, hs])
        p = jnp.exp(s - jnp.max(s, axis=-1, keepdims=True))
        l = jnp.sum(p, axis=-1, keepdims=True)
        outs.append((_dot(p.astype(BF16), mv16[:, hs]) / l).astype(BF16))
    o_ref[...] = x + _dot(jnp.concatenate(outs, axis=1), wo_ref[...])


def _xattn(x2d, bsz, L, mk, mv, g, wq, wo):
    tq = min(512, L)
    nq = L // tq
    row = pl.BlockSpec((tq, D_MODEL), lambda b, t: (b * nq + t, 0))
    mem = pl.BlockSpec((1, N_MEM, D_MODEL), lambda b, t: (b, 0, 0))
    full = lambda shape: pl.BlockSpec(shape, lambda b, t: (0, 0))
    return pl.pallas_call(
        _xattn_kernel,
        grid=(bsz, nq),
        in_specs=[row, mem, mem, full((1, D_MODEL)), full((D_MODEL, D_MODEL)), full((D_MODEL, D_MODEL))],
        out_specs=row,
        out_shape=SDS((bsz * L, D_MODEL), F32),
        scratch_shapes=[pltpu.VMEM((N_MEM, D_MODEL), BF16)] * 2,
        compiler_params=_cparams(("parallel", "arbitrary")),
        name="xattn",
    )(x2d, mk, mv, g, wq, wo)


def _ffn_kernel(*refs, final_norm):
    if final_norm:
        x_ref, g_ref, wu_ref, wd_ref, cw_ref, cb_ref, fst_ref, gf_ref, o_ref, fst_o_ref, halo_scr = refs
    else:
        x_ref, g_ref, wu_ref, wd_ref, cw_ref, cb_ref, fst_ref, o_ref, fst_o_ref, halo_scr = refs

    @pl.when(pl.program_id(1) == 0)
    def _():
        halo_scr[...] = fst_ref[0]

    x = x_ref[...]
    hf = _rms(x, g_ref[...]).astype(BF16)
    acc = None
    for c in range(D_FF // FFN_CHUNK):
        cs = slice(c * FFN_CHUNK, (c + 1) * FFN_CHUNK)
        u = _dot(hf, wu_ref[:, cs])
        gpre = _dot(hf, wu_ref[:, D_FF + c * FFN_CHUNK:D_FF + (c + 1) * FFN_CHUNK])
        conv, new_halo = _causal_conv(halo_scr[:, cs], gpre, cw_ref[:, cs], F_CONV)
        halo_scr[:, cs] = new_halo
        act = (u * _silu(conv + cb_ref[:, cs])).astype(BF16)
        contrib = _dot(act, wd_ref[cs, :])
        acc = contrib if acc is None else acc + contrib
    fst_o_ref[0, 0] = halo_scr[...]
    y = x + acc
    o_ref[...] = _rms(y, gf_ref[...]) if final_norm else y


def _ffn(x2d, bsz, L, g, wup, wdn, cw, cb, fst, gfinal):
    tr = min(FFN_ROWS_PER_STEP, L)
    nt = L // tr
    assert L % tr == 0 and D_FF % FFN_CHUNK == 0
    final_norm = gfinal is not None
    row = pl.BlockSpec((tr, D_MODEL), lambda b, t: (b * nt + t, 0))
    const = lambda shape: pl.BlockSpec(shape, lambda b, t: (0, 0), pipeline_mode=pl.Buffered(1))
    in_specs = [
        row, const((1, D_MODEL)), const((D_MODEL, 2 * D_FF)), const((D_FF, D_MODEL)),
        const((F_CONV, D_FF)), const((1, D_FF)),
        pl.BlockSpec((1, SUBLANES, D_FF), lambda b, t: (b, 0, 0)),
    ]
    args = [x2d, g, wup, wdn, cw, cb, fst]
    if final_norm:
        in_specs.append(const((1, D_MODEL)))
        args.append(gfinal)
    return pl.pallas_call(
        functools.partial(_ffn_kernel, final_norm=final_norm),
        grid=(bsz, nt),
        in_specs=in_specs,
        out_specs=[row, pl.BlockSpec((1, 1, SUBLANES, D_FF), lambda b, t: (b, t, 0, 0))],
        out_shape=[SDS((bsz * L, D_MODEL), F32), SDS((bsz, nt, SUBLANES, D_FF), F32)],
        scratch_shapes=[pltpu.VMEM((SUBLANES, D_FF), F32)],
        compiler_params=_cparams(("parallel", "arbitrary")),
        name="ffn",
    )(*args)


def _pad_state(st):
    return jnp.pad(st.astype(F32), ((0, 0), (SUBLANES - st.shape[1], 0), (0, 0)))


def _head_row(v, width):
    return jnp.repeat(v.astype(F32), width)[None, :]


def _prep_layer(l, p):
    w_in = p['w_in'][l]
    w_main = jnp.concatenate([w_in[:, a:b] for a, b in _SEGS_MAIN], axis=1).astype(BF16)
    n_small = sum(b - a for a, b in _SEGS_SMALL)
    w_small = jnp.concatenate([w_in[:, a:b] for a, b in _SEGS_SMALL] + [jnp.zeros((D_MODEL, LANES - n_small), F32)],
                              axis=1).astype(BF16)
    prm = jnp.zeros((SUBLANES, LANES), F32)
    prm = prm.at[0, SM_DEC:SM_DEC + B_HEADS].set(p['b_dt_bias'][l])
    prm = prm.at[0, SM_DT:SM_DT + C_HEADS].set(p['c_dt_bias'][l])
    prm = prm.at[1, SM_DEC:SM_DEC + B_HEADS].set(p['b_a_log'][l])
    prm = prm.at[1, SM_DT:SM_DT + C_HEADS].set(p['c_a_log'][l])
    rows = jnp.zeros((SUBLANES, C_INNER), F32)
    rows = rows.at[0].set(_head_row(p['c_dt_bias'][l], C_HEAD_DIM)[0])
    rows = rows.at[1].set(_head_row(p['c_a_log'][l], C_HEAD_DIM)[0])
    rows = rows.at[2].set(_head_row(p['c_d'][l], C_HEAD_DIM)[0])
    rows = rows.at[3].set(p['c_norm'][l].astype(F32))
    return dict(
        norm_mix=p['norm_mix'][l][None, :], w_main=w_main, w_small=w_small,
        rel=p['a_rel_bias'][l],
        b_conv_w=p['b_conv_w'][l], prm=prm, b_norm=p['b_norm'][l][None, :],
        c_conv_w=p['c_conv_w'][l], c_conv_b=p['c_conv_b'][l][None, :], rows=rows,
        wa=p['w_br_a'][l].astype(BF16), wb=p['w_br_b'][l].astype(BF16), wc=p['w_br_c'][l].astype(BF16),
        wo=p['w_out'][l].astype(BF16),
        norm_x=p['norm_x'][l][None, :], norm_mem=p['norm_mem'][l][None, :],
        wxq=p['wx_q'][l].astype(BF16), wxk=p['wx_k'][l].astype(BF16), wxv=p['wx_v'][l].astype(BF16),
        wxo=p['wx_o'][l].astype(BF16),
        norm_ffn=p['norm_ffn'][l][None, :], w_up=p['w_up'][l].astype(BF16), w_down=p['w_down'][l].astype(BF16),
        f_conv_w=p['f_conv_w'][l], f_conv_b=p['f_conv_b'][l][None, :],
    )


def _expand_matrix():
    e = np.zeros((LANES, C_INNER), np.float32)
    for h in range(C_HEADS):
        e[SM_DT + h, h * C_HEAD_DIM:(h + 1) * C_HEAD_DIM] = 1.0
    return jnp.asarray(e)


def _layer(x2d, bsz, L, lw, a_past, b_conv, b_rec, c_conv, c_ssm, f_conv, mk, mv, gfinal, expand):
    z, zs = _in_proj(x2d, lw['norm_mix'], lw['w_main'], lw['w_small'])

    C = min(CHUNK, L)
    G = 2 if L // C >= 2 else 1
    bias = _attn_bias(lw['rel'], C, G)
    past = None
    if a_past is not None:
        past = tuple(t.reshape(bsz, A_WINDOW, A_WIDTH) for t in a_past)
    ya = _band_attn(z, bsz, L, C, G, bias, past)
    keep = min(A_WINDOW, L)
    z3 = z.reshape(bsz, L, NZ)
    a_k = z3[:, L - keep:, A_WIDTH:2 * A_WIDTH].astype(F32).reshape(bsz, keep, A_HEADS, A_HEAD_DIM)
    a_v = z3[:, L - keep:, 2 * A_WIDTH:3 * A_WIDTH].astype(F32).reshape(bsz, keep, A_HEADS, A_HEAD_DIM)

    yb, b_conv_new, b_rec_new = _gdn(z, zs, bsz, L, _pad_state(b_conv), b_rec.astype(F32),
                                     lw['b_conv_w'], lw['prm'], lw['b_norm'])

    sst = jnp.transpose(c_ssm.astype(F32).reshape(bsz, C_GROUPS, C_HEADS // C_GROUPS, C_HEAD_DIM, C_STATE),
                        (0, 1, 4, 2, 3)).reshape(bsz, C_GROUPS, C_STATE, C_GROUP_W)
    yc, c_conv_new, sst_new = _ssd(z, zs, bsz, L, _pad_state(c_conv), sst, lw['c_conv_w'], lw['c_conv_b'],
                                   expand, lw['rows'])
    c_ssm_new = jnp.transpose(sst_new.reshape(bsz, C_GROUPS, C_STATE, C_HEADS // C_GROUPS, C_HEAD_DIM),
                              (0, 1, 3, 4, 2)).reshape(bsz, C_HEADS, C_HEAD_DIM, C_STATE)

    x2d = _merge(x2d, ya, yb, yc, z, lw['wa'], lw['wb'], lw['wc'], lw['wo'])
    x2d = _xattn(x2d, bsz, L, mk, mv, lw['norm_x'], lw['wxq'], lw['wxo'])

    x2d, fst_new = _ffn(x2d, bsz, L, lw['norm_ffn'], lw['w_up'], lw['w_down'], lw['f_conv_w'], lw['f_conv_b'],
                        _pad_state(f_conv), gfinal)
    f_conv_new = fst_new[:, -1, SUBLANES - (F_CONV - 1):, :]
    return x2d, (a_k, a_v, b_conv_new, b_rec_new, c_conv_new, c_ssm_new, f_conv_new)


def kernel(x_prompt, x_sample, cache_attn_k, cache_attn_v, state_b_conv, state_b_rec, state_c_conv, state_c_ssm, state_ffn_conv, cache_mem_k, cache_mem_v, mem_prompt, norm_mix, w_in, a_rel_bias, b_conv_w, b_a_log, b_dt_bias, b_norm, c_conv_w, c_conv_b, c_dt_bias, c_a_log, c_d, c_norm, w_br_a, w_br_b, w_br_c, w_out, norm_x, norm_mem, wx_q, wx_k, wx_v, wx_o, norm_ffn, w_up, f_conv_w, f_conv_b, w_down, norm_final):
    params = dict(norm_mix=norm_mix, w_in=w_in, a_rel_bias=a_rel_bias, b_conv_w=b_conv_w, b_a_log=b_a_log,
                  b_dt_bias=b_dt_bias, b_norm=b_norm, c_conv_w=c_conv_w, c_conv_b=c_conv_b, c_dt_bias=c_dt_bias,
                  c_a_log=c_a_log, c_d=c_d, c_norm=c_norm, w_br_a=w_br_a, w_br_b=w_br_b, w_br_c=w_br_c,
                  w_out=w_out, norm_x=norm_x, norm_mem=norm_mem, wx_q=wx_q, wx_k=wx_k, wx_v=wx_v, wx_o=wx_o,
                  norm_ffn=norm_ffn, w_up=w_up, f_conv_w=f_conv_w, f_conv_b=f_conv_b, w_down=w_down)
    nb, seq, _ = x_prompt.shape
    db, dseq, _ = x_sample.shape
    expand = _expand_matrix()
    gfin = norm_final[None, :]
    xp = x_prompt.reshape(nb * seq, D_MODEL)
    xs = x_sample.reshape(db * dseq, D_MODEL)
    p_states, s_states, p_mk, p_mv = [], [], [], []
    for l in range(DEPTH):
        lw = _prep_layer(l, params)
        last = gfin if l == DEPTH - 1 else None
        mk, mv = _memory_kv(mem_prompt, lw['norm_mem'], lw['wxk'], lw['wxv'])
        xp, st_p = _layer(
            xp, nb, seq, lw, None,
            jnp.zeros((nb, B_CONV - 1, 3 * B_WIDTH), F32),
            jnp.zeros((nb, B_HEADS, B_HEAD_DIM, B_HEAD_DIM), F32),
            jnp.zeros((nb, C_CONV - 1, C_XBC), F32),
            jnp.zeros((nb, C_HEADS, C_HEAD_DIM, C_STATE), F32),
            jnp.zeros((nb, F_CONV - 1, D_FF), F32),
            mk, mv, last, expand)
        p_states.append(st_p)
        p_mk.append(mk.reshape(nb, N_MEM, X_HEADS, X_HEAD_DIM))
        p_mv.append(mv.reshape(nb, N_MEM, X_HEADS, X_HEAD_DIM))
        xs, st_s = _layer(
            xs, db, dseq, lw, (cache_attn_k[l], cache_attn_v[l]),
            state_b_conv[l], state_b_rec[l], state_c_conv[l], state_c_ssm[l], state_ffn_conv[l],
            cache_mem_k[l].reshape(db, N_MEM, D_MODEL), cache_mem_v[l].reshape(db, N_MEM, D_MODEL),
            last, expand)
        s_states.append(st_s)

    y_prompt = xp.reshape(nb, seq, D_MODEL)
    y_sample = xs.reshape(db, dseq, D_MODEL)
    pst = [jnp.stack([s[i] for s in p_states]) for i in range(7)]
    sst = [jnp.stack([s[i] for s in s_states]) for i in range(7)]
    return (y_prompt, y_sample, pst[0], pst[1], pst[2], pst[3], pst[4], pst[5], pst[6],
            jnp.stack(p_mk), jnp.stack(p_mv),
            sst[0], sst[1], sst[2], sst[3], sst[4], sst[5], sst[6])
```

```python
import functools

import jax
import jax.numpy as jnp
import numpy as np
from jax import lax
from jax.experimental import pallas as pl
from jax.experimental.pallas import tpu as pltpu

F32 = jnp.float32
BF16 = jnp.bfloat16
SDS = jax.ShapeDtypeStruct

D_MODEL = 1024
DEPTH = 2
CHUNK = 64
N_MEM = 256
EPS = 1e-6

A_HEADS = 8
A_HEAD_DIM = 64
A_WIDTH = A_HEADS * A_HEAD_DIM
A_WINDOW = 8 * CHUNK
A_MAX_REL = 128

B_HEADS = 4
B_HEAD_DIM = 128
B_WIDTH = B_HEADS * B_HEAD_DIM
B_CONV = 4

C_HEADS = 16
C_HEAD_DIM = 64
C_INNER = C_HEADS * C_HEAD_DIM
C_GROUPS = 2
C_STATE = 128
C_XBC = C_INNER + 2 * C_GROUPS * C_STATE
C_CONV = 4
C_GROUP_W = C_INNER // C_GROUPS

X_HEADS = 4
X_HEAD_DIM = D_MODEL // X_HEADS

D_FF = 2816
F_CONV = 3

LANES = 128
SUBLANES = 8
NEG = -1e30
FFN_ROWS_PER_STEP = 1024
FFN_CHUNK = 256
GDN_SEQS_PER_STEP = 4
GDN_ROWS_PER_STEP = 512
ATTN_HEADS_PER_STEP = 4
VMEM_LIMIT = 56 * 1024 * 1024

NZ = 9216
ZW_A = A_WIDTH
ZB_BQKV = 1
ZB_CZ = 3
ZB_CX = 4
ZB_CBC = 10
ZB_BGATE = 11
ZB_GATES = 6
SM_BETA = 0
SM_DEC = 4
SM_DT = 8

_O_BQKV = 3 * A_WIDTH
_O_BETA = _O_BQKV + 3 * B_WIDTH
_O_DEC = _O_BETA + B_HEADS
_O_BGATE = _O_DEC + B_HEADS
_O_CZ = _O_BGATE + B_WIDTH
_O_CXBC = _O_CZ + C_INNER
_O_CDT = _O_CXBC + C_XBC
_O_GATES = _O_CDT + C_HEADS
_SEGS_MAIN = ((0, _O_BETA), (_O_CZ, _O_CXBC), (_O_CXBC, _O_CDT), (_O_BGATE, _O_CZ), (_O_GATES, _O_GATES + 3 * D_MODEL))
_SEGS_SMALL = ((_O_BETA, _O_BGATE), (_O_CDT, _O_GATES))


def _cparams(sem):
    return pltpu.CompilerParams(dimension_semantics=sem, vmem_limit_bytes=VMEM_LIMIT)


def _rms(x, g):
    return x * lax.rsqrt(jnp.mean(x * x, axis=-1, keepdims=True) + EPS) * g


def _silu(x):
    return x * jax.nn.sigmoid(x)


def _softplus(x):
    return jnp.maximum(x, 0.0) + jnp.log1p(jnp.exp(-jnp.abs(x)))


def _dot(a, b):
    return jnp.dot(a, b, preferred_element_type=F32)


def _pieces(a, n):
    out = []
    for _ in range(n - 1):
        p = a.astype(BF16)
        out.append(p)
        a = a - p.astype(F32)
    out.append(a.astype(BF16))
    return out


def _dot_sel(sel, b):
    s16 = sel.astype(BF16)
    b1, b2, b3 = _pieces(b, 3)
    return _dot(s16, b1) + (_dot(s16, b2) + _dot(s16, b3))


def _dot_pick(a, sel):
    s16 = sel.astype(BF16)
    a1, a2, a3 = _pieces(a, 3)
    return _dot(a1, s16) + (_dot(a2, s16) + _dot(a3, s16))


def _dot_x3(a, b):
    ah, al = _pieces(a, 2)
    bh, bl = _pieces(b, 2)
    return _dot(ah, bh) + (_dot(ah, bl) + _dot(al, bh))


def _dot_nt(a, b):
    return lax.dot_general(a, b, (((1,), (1,)), ((), ())), preferred_element_type=F32)


def _dot_tn(a, b):
    return lax.dot_general(a, b, (((0,), (0,)), ((), ())), preferred_element_type=F32)


def _row_start(i, n):
    return i * n if isinstance(i, int) else pl.multiple_of(i * n, n)


def _causal_conv(prev8, cur, w_ref, taps):
    ext = jnp.concatenate([prev8, cur], axis=0)
    acc = cur * w_ref[taps - 1:taps, :]
    for s in range(1, taps):
        acc = acc + pltpu.roll(ext, s, 0)[SUBLANES:, :] * w_ref[taps - 1 - s:taps - s, :]
    return acc, ext[cur.shape[0]:, :]


def _inproj_kernel(x_ref, g_ref, w_ref, ws_ref, z_ref, zs_ref, h_scr):
    @pl.when(pl.program_id(1) == 0)
    def _():
        hb = _rms(x_ref[...], g_ref[...]).astype(BF16)
        h_scr[...] = hb
        zs_ref[...] = _dot(hb, ws_ref[...])

    z_ref[...] = _dot(h_scr[...], w_ref[...]).astype(BF16)


def _in_proj(x2d, g, w_main, w_small):
    T = x2d.shape[0]
    tm = min(1024, T)
    tn = 1536
    return pl.pallas_call(
        _inproj_kernel,
        grid=(T // tm, NZ // tn),
        in_specs=[
            pl.BlockSpec((tm, D_MODEL), lambda i, j: (i, 0)),
            pl.BlockSpec((1, D_MODEL), lambda i, j: (0, 0)),
            pl.BlockSpec((D_MODEL, tn), lambda i, j: (0, j)),
            pl.BlockSpec((D_MODEL, LANES), lambda i, j: (0, 0)),
        ],
        out_specs=[
            pl.BlockSpec((tm, tn), lambda i, j: (i, j)),
            pl.BlockSpec((tm, LANES), lambda i, j: (i, 0)),
        ],
        out_shape=[SDS((T, NZ), BF16), SDS((T, LANES), F32)],
        scratch_shapes=[pltpu.VMEM((tm, D_MODEL), BF16)],
        compiler_params=_cparams(("parallel", "arbitrary")),
        name="in_proj",
    )(x2d, g, w_main, w_small)


def _attn_kernel(*refs, L, C, G, has_past):
    if has_past:
        q_ref, k_ref, v_ref, pk_ref, pv_ref, bias_ref, o_ref, kx, vx = refs
        kx[0:A_WINDOW, :] = pk_ref[0].astype(BF16)
        vx[0:A_WINDOW, :] = pv_ref[0].astype(BF16)
    else:
        q_ref, k_ref, v_ref, bias_ref, o_ref, kx, vx = refs
        kx[0:A_WINDOW, :] = jnp.zeros((A_WINDOW, A_WIDTH), BF16)
        vx[0:A_WINDOW, :] = jnp.zeros((A_WINDOW, A_WIDTH), BF16)
    kx[A_WINDOW:A_WINDOW + L, :] = k_ref[...]
    vx[A_WINDOW:A_WINDOW + L, :] = v_ref[...]

    GC = G * C
    NB = A_WINDOW + GC
    lane = lax.broadcasted_iota(jnp.int32, (1, LANES), 1)
    first_head = lane < A_HEAD_DIM
    col = lax.broadcasted_iota(jnp.int32, (1, NB), 1)
    scale = A_HEAD_DIM ** -0.5

    def group(g, carry, masked):
        r0 = _row_start(g, GC)
        q = q_ref[pl.ds(r0, GC), :] * jnp.asarray(scale, BF16)
        kb = kx[pl.ds(r0, NB), :]
        vb = vx[pl.ds(r0, NB), :]
        for h0 in range(0, A_HEADS, ATTN_HEADS_PER_STEP):
            hs = range(h0, h0 + ATTN_HEADS_PER_STEP)
            sl = {h: slice((h // 2) * LANES, (h // 2 + 1) * LANES) for h in hs}
            msk = {h: first_head if h % 2 == 0 else jnp.logical_not(first_head) for h in hs}
            s = {h: _dot_nt(jnp.where(msk[h], q[:, sl[h]], jnp.zeros((GC, LANES), BF16)), kb[:, sl[h]]) for h in hs}
            s = {h: s[h] + bias_ref[h] for h in hs}
            if masked:
                s = {h: jnp.where(r0 + col >= A_WINDOW, s[h], NEG) for h in hs}
            p = {h: jnp.exp(s[h] - jnp.max(s[h], axis=-1, keepdims=True)) for h in hs}
            l = {h: jnp.sum(p[h], axis=-1, keepdims=True) for h in hs}
            pv = {h: _dot(p[h].astype(BF16), vb[:, sl[h]]) / l[h] for h in hs}
            for h in hs:
                if h % 2 == 1:
                    o_ref[pl.ds(r0, GC), sl[h]] = jnp.where(first_head, pv[h - 1], pv[h]).astype(BF16)
        return carry

    ng = L // GC
    n_masked = 0 if has_past else min(ng, -(-A_WINDOW // GC))
    if ng == 1:
        group(0, 0, n_masked > 0)
    else:
        if n_masked:
            lax.fori_loop(0, n_masked, functools.partial(group, masked=True), 0)
        if ng > n_masked:
            lax.fori_loop(n_masked, ng, functools.partial(group, masked=False), 0)


def _attn_bias(table, C, G):
    GC = G * C
    NB = A_WINDOW + GC
    r = np.arange(GC)[:, None]
    j = np.arange(NB)[None, :]
    lo = (r // C) * C
    allowed = (j >= lo) & (j < lo + A_WINDOW + C)
    dmax = A_WINDOW + GC - 1
    rel = np.clip(dmax - np.arange(NB + GC - 1), -A_MAX_REL, A_MAX_REL) + A_MAX_REL
    vflip = table[rel].astype(F32).T
    b = jnp.stack([vflip[:, GC - 1 - i:GC - 1 - i + NB] for i in range(GC)], axis=1)
    return jnp.where(allowed[None], b, NEG)


def _band_attn(z, bsz, L, C, G, bias, past):
    T = bsz * L
    has_past = past is not None
    NB = A_WINDOW + G * C
    in_specs = [pl.BlockSpec((L, A_WIDTH), lambda b, i=i: (b, i)) for i in range(3)]
    args = [z, z, z]
    if has_past:
        in_specs += [pl.BlockSpec((1, A_WINDOW, A_WIDTH), lambda b: (b, 0, 0))] * 2
        args += list(past)
    in_specs.append(pl.BlockSpec((A_HEADS, G * C, NB), lambda b: (0, 0, 0)))
    args.append(bias)
    return pl.pallas_call(
        functools.partial(_attn_kernel, L=L, C=C, G=G, has_past=has_past),
        grid=(bsz,),
        in_specs=in_specs,
        out_specs=pl.BlockSpec((L, A_WIDTH), lambda b: (b, 0)),
        out_shape=SDS((T, A_WIDTH), BF16),
        scratch_shapes=[pltpu.VMEM((A_WINDOW + L, A_WIDTH), BF16)] * 2,
        compiler_params=_cparams(("parallel",)),
        name="band_attn",
    )(*args)


def _gdn_kernel(qkv_ref, sm_ref, gate_ref, cst_ref, rst_ref, cw_ref, prm_ref, bn_ref,
                yb_ref, cst_o_ref, rst_o_ref, s_scr, prev_scr, *, nb, lb, cl):
    t = pl.program_id(1)
    nc = lb // cl

    @pl.when(t == 0)
    def _():
        for n in range(nb):
            for h in range(B_HEADS):
                s_scr[n * B_HEADS + h] = rst_ref[n, h]
        prev_scr[...] = cst_ref[...]

    bias_row = prm_ref[0:1, :]
    aneg_row = -jnp.exp(prm_ref[1:2, :])
    bn = bn_ref[...]
    ri = lax.broadcasted_iota(jnp.int32, (cl, cl), 0)
    ci = lax.broadcasted_iota(jnp.int32, (cl, cl), 1)
    incl = ri >= ci
    strict = ri > ci
    tril = incl.astype(F32)
    eye = (ri == ci).astype(F32)
    sh = int(np.log2(SUBLANES))
    diag_blk = (ri >> sh) == (ci >> sh)
    sub_blk = []
    while (1 << sh) < cl:
        sub_blk.append(((ri >> (sh + 1)) == (ci >> (sh + 1))) & (((ri >> sh) & 1) == 1) & (((ci >> sh) & 1) == 0))
        sh += 1
    U = [(n, h) for n in range(nb) for h in range(B_HEADS)]

    def body(c, carry):
        r0 = _row_start(c, cl)
        rows = pl.ds(r0, cl)
        act, beta_all, gv = [], [], []
        for n in range(nb):
            conv, new_prev = _causal_conv(prev_scr[n], qkv_ref[n, rows, :].astype(F32), cw_ref, B_CONV)
            prev_scr[n] = new_prev
            act.append(_silu(conv))
            smc = sm_ref[n, rows, :]
            beta_all.append(jax.nn.sigmoid(smc))
            gv.append(_softplus(smc + bias_row) * aneg_row)
        gcs = [_dot_sel(tril, gv[n]) for n in range(nb)]
        q = {(n, h): act[n][:, h * B_HEAD_DIM:(h + 1) * B_HEAD_DIM] for n, h in U}
        k = {(n, h): act[n][:, B_WIDTH + h * B_HEAD_DIM:B_WIDTH + (h + 1) * B_HEAD_DIM] for n, h in U}
        v = {(n, h): act[n][:, 2 * B_WIDTH + h * B_HEAD_DIM:2 * B_WIDTH + (h + 1) * B_HEAD_DIM] for n, h in U}
        q = {u: q[u] * lax.rsqrt(jnp.sum(q[u] * q[u], axis=-1, keepdims=True) + EPS) * (B_HEAD_DIM ** -0.5) for u in U}
        k = {u: k[u] * lax.rsqrt(jnp.sum(k[u] * k[u], axis=-1, keepdims=True) + EPS) for u in U}
        beta = {(n, h): beta_all[n][:, SM_BETA + h:SM_BETA + h + 1] for n, h in U}
        g = {(n, h): gv[n][:, SM_DEC + h:SM_DEC + h + 1] for n, h in U}
        gc = {(n, h): gcs[n][:, SM_DEC + h:SM_DEC + h + 1] for n, h in U}
        gl = {(n, h): gcs[n][cl - 1:cl, SM_DEC + h:SM_DEC + h + 1] for n, h in U}
        e = {u: _dot_sel(tril, jnp.where(strict, g[u], 0.0)) for u in U}
        decay = {u: jnp.where(incl, jnp.exp(e[u]), 0.0) for u in U}
        kb = {u: k[u] * beta[u] for u in U}
        k16 = {u: k[u].astype(BF16) for u in U}
        a = {u: _dot_nt(kb[u].astype(BF16), k16[u]) for u in U}
        lm = {u: jnp.where(strict, a[u] * decay[u], 0.0) for u in U}
        m = {u: jnp.where(diag_blk, -lm[u], 0.0) for u in U}
        p = {u: eye + m[u] for u in U}
        for _ in range(2):
            m = {u: _dot_x3(m[u], m[u]) for u in U}
            p = {u: p[u] + _dot_x3(p[u], m[u]) for u in U}
        for cmask in sub_blk:
            cx = {u: _dot_x3(jnp.where(cmask, lm[u], 0.0), p[u]) for u in U}
            p = {u: p[u] - _dot_x3(p[u], cx[u]) for u in U}
        egc = {u: jnp.exp(gc[u]) for u in U}
        sol = {u: _dot_x3(p[u], jnp.concatenate([v[u] * beta[u], kb[u] * egc[u]], axis=1)) for u in U}
        qk = {u: _dot_nt(q[u].astype(BF16), k16[u]) * decay[u] for u in U}
        s0 = {(n, h): s_scr[n * B_HEADS + h] for n, h in U}
        s16 = {u: s0[u].astype(BF16) for u in U}
        uu = {u: sol[u][:, :B_HEAD_DIM] - _dot(sol[u][:, B_HEAD_DIM:].astype(BF16), s16[u]) for u in U}
        u16 = {u: uu[u].astype(BF16) for u in U}
        kt = {u: (k[u] * jnp.exp(gl[u] - gc[u])).astype(BF16) for u in U}
        snew = {u: s0[u] * jnp.exp(gl[u]) + _dot_tn(kt[u], u16[u]) for u in U}
        o = {u: _dot((q[u] * egc[u]).astype(BF16), s16[u]) + _dot(qk[u].astype(BF16), u16[u]) for u in U}
        for n, h in U:
            hs = slice(h * B_HEAD_DIM, (h + 1) * B_HEAD_DIM)
            s_scr[n * B_HEADS + h] = snew[n, h]
            gate = gate_ref[n, rows, hs].astype(F32)
            yb_ref[n, rows, hs] = (_rms(o[n, h], bn) * _silu(gate)).astype(BF16)
        return carry

    if nc == 1:
        body(0, 0)
    else:
        lax.fori_loop(0, nc, body, 0)

    @pl.when(t == pl.num_programs(1) - 1)
    def _():
        cst_o_ref[...] = prev_scr[:, SUBLANES - (B_CONV - 1):, :]
        for n in range(nb):
            for h in range(B_HEADS):
                rst_o_ref[n, h] = s_scr[n * B_HEADS + h]


def _gdn(z, zs, bsz, L, cst, rst, cw, prm, bn):
    cl = min(CHUNK, L)
    nb = GDN_SEQS_PER_STEP
    lb = min(GDN_ROWS_PER_STEP, L)
    assert bsz % nb == 0 and L % lb == 0 and lb % cl == 0
    z3 = z.reshape(bsz, L, NZ)
    zs3 = zs.reshape(bsz, L, LANES)
    yb, cst_o, rst_o = pl.pallas_call(
        functools.partial(_gdn_kernel, nb=nb, lb=lb, cl=cl),
        grid=(bsz // nb, L // lb),
        in_specs=[
            pl.BlockSpec((nb, lb, 3 * B_WIDTH), lambda b, t: (b, t, ZB_BQKV)),
            pl.BlockSpec((nb, lb, LANES), lambda b, t: (b, t, 0)),
            pl.BlockSpec((nb, lb, B_WIDTH), lambda b, t: (b, t, ZB_BGATE)),
            pl.BlockSpec((nb, SUBLANES, 3 * B_WIDTH), lambda b, t: (b, 0, 0)),
            pl.BlockSpec((nb, B_HEADS, B_HEAD_DIM, B_HEAD_DIM), lambda b, t: (b, 0, 0, 0)),
            pl.BlockSpec((B_CONV, 3 * B_WIDTH), lambda b, t: (0, 0)),
            pl.BlockSpec((SUBLANES, LANES), lambda b, t: (0, 0)),
            pl.BlockSpec((1, B_HEAD_DIM), lambda b, t: (0, 0)),
        ],
        out_specs=[
            pl.BlockSpec((nb, lb, B_WIDTH), lambda b, t: (b, t, 0)),
            pl.BlockSpec((nb, B_CONV - 1, 3 * B_WIDTH), lambda b, t: (b, 0, 0)),
            pl.BlockSpec((nb, B_HEADS, B_HEAD_DIM, B_HEAD_DIM), lambda b, t: (b, 0, 0, 0)),
        ],
        out_shape=[
            SDS((bsz, L, B_WIDTH), BF16),
            SDS((bsz, B_CONV - 1, 3 * B_WIDTH), F32),
            SDS((bsz, B_HEADS, B_HEAD_DIM, B_HEAD_DIM), F32),
        ],
        scratch_shapes=[
            pltpu.VMEM((nb * B_HEADS, B_HEAD_DIM, B_HEAD_DIM), F32),
            pltpu.VMEM((nb, SUBLANES, 3 * B_WIDTH), F32),
        ],
        compiler_params=_cparams(("parallel", "arbitrary")),
        name="gdn",
    )(z3, zs3, z3, cst, rst, cw, prm, bn)
    return yb.reshape(bsz * L, B_WIDTH), cst_o, rst_o


def _ssd_kernel(cx_ref, cbc_ref, cz_ref, sm_ref, cst_ref, sst_ref, cwx_ref, cwbc_ref, cbx_ref, cbbc_ref,
                expand_ref, prm_ref, rows_ref, yc_ref, cst_o_ref, sst_o_ref, h_scr, *, L, cl):
    nc = L // cl
    h_scr[...] = sst_ref[0]
    bias_row = prm_ref[0:1, :]
    aneg_row = -jnp.exp(prm_ref[1:2, :])
    cdx = rows_ref[2:3, :]
    cn = rows_ref[3:4, :]
    ri = lax.broadcasted_iota(jnp.int32, (cl, cl), 0)
    ci = lax.broadcasted_iota(jnp.int32, (cl, cl), 1)
    tril = (ri >= ci).astype(F32)
    rx = lax.broadcasted_iota(jnp.int32, (cl, C_INNER), 0)
    jx = lax.broadcasted_iota(jnp.int32, (cl, C_INNER), 1) & (C_HEAD_DIM - 1)
    inclx = rx >= jx
    lane = lax.broadcasted_iota(jnp.int32, (1, LANES), 1)
    first_head = lane < C_HEAD_DIM

    def pad_rows(a):
        if cl == CHUNK:
            return a
        return jnp.concatenate([a, jnp.zeros((CHUNK - cl, a.shape[1]), a.dtype)], axis=0)

    def pad_lanes(a):
        if cl == C_HEAD_DIM:
            return a
        return jnp.concatenate([a, jnp.zeros((a.shape[0], C_HEAD_DIM - cl), a.dtype)], axis=1)

    def body(c, carry):
        px, pbc = carry
        r0 = _row_start(c, cl)
        convx, npx = _causal_conv(px, cx_ref[pl.ds(r0, cl), :].astype(F32), cwx_ref, C_CONV)
        convbc, npbc = _causal_conv(pbc, cbc_ref[pl.ds(r0, cl), :].astype(F32), cwbc_ref, C_CONV)
        xs = _silu(convx + cbx_ref[...])
        bcs = _silu(convbc + cbbc_ref[...])
        dt_c = _softplus(sm_ref[pl.ds(r0, cl), :] + bias_row)
        ac_c = _dot_sel(tril, dt_c * aneg_row)
        dtx = _dot_pick(dt_c, expand_ref[...])
        acx = _dot_pick(ac_c, expand_ref[...])
        ac_t = ac_c.T
        rowsj = [pad_lanes(ac_t[SM_DT + h:SM_DT + h + 1, :]) for h in range(C_HEADS)]
        ac_row = jnp.concatenate(rowsj, axis=1)
        decayx = jnp.exp(jnp.where(inclx, acx - ac_row, NEG))
        alast = acx[cl - 1:cl, :]
        eac = jnp.exp(acx)
        ealast = jnp.exp(alast)
        xdt = xs * dtx
        xtil = (xdt * jnp.exp(alast - acx)).astype(BF16)
        ys = []
        for g in range(C_GROUPS):
            gs = slice(g * C_GROUP_W, (g + 1) * C_GROUP_W)
            bg = bcs[:, g * C_STATE:(g + 1) * C_STATE].astype(BF16)
            cg = bcs[:, (C_GROUPS + g) * C_STATE:(C_GROUPS + g + 1) * C_STATE].astype(BF16)
            brep = jnp.concatenate([pad_rows(bg)] * (C_GROUP_W // CHUNK), axis=0)
            w = (_dot_nt(cg, brep) * decayx[:, gs]).astype(BF16)
            hg = h_scr[g]
            yoff = _dot(cg, hg.astype(BF16)) * eac[:, gs]
            yd = []
            for pr in range(C_GROUP_W // LANES):
                lo = g * C_GROUP_W + pr * LANES
                xp = pad_rows(xdt[:, lo:lo + LANES])
                bd = jnp.concatenate([jnp.where(first_head, xp, 0.0), jnp.where(first_head, 0.0, xp)], axis=0)
                yd.append(_dot(w[:, pr * LANES:(pr + 1) * LANES], bd.astype(BF16)))
            h_scr[g] = hg * ealast[:, gs] + _dot_tn(bg, xtil[:, gs])
            ys.append(jnp.concatenate(yd, axis=1) + yoff)
        y = jnp.concatenate(ys, axis=1) + cdx * xs
        t = y * _silu(cz_ref[pl.ds(r0, cl), :].astype(F32))
        yc_ref[pl.ds(r0, cl), :] = _rms(t, cn).astype(BF16)
        return npx, npbc

    carry = (cst_ref[0, :, :C_INNER], cst_ref[0, :, C_INNER:])
    carry = body(0, carry) if nc == 1 else lax.fori_loop(0, nc, body, carry)
    cst_o_ref[0, :, :C_INNER] = carry[0][SUBLANES - (C_CONV - 1):, :]
    cst_o_ref[0, :, C_INNER:] = carry[1][SUBLANES - (C_CONV - 1):, :]
    sst_o_ref[0] = h_scr[...]


def _ssd(z, zs, bsz, L, cst, sst, cw, cb, expand, prm, rows):
    T = bsz * L
    cl = min(CHUNK, L)
    nbc = C_XBC - C_INNER
    full = lambda shape: pl.BlockSpec(shape, lambda b: (0,) * len(shape))
    return pl.pallas_call(
        functools.partial(_ssd_kernel, L=L, cl=cl),
        grid=(bsz,),
        in_specs=[
            pl.BlockSpec((L, C_INNER), lambda b: (b, ZB_CX)),
            pl.BlockSpec((L, nbc), lambda b: (b, ZB_CBC)),
            pl.BlockSpec((L, C_INNER), lambda b: (b, ZB_CZ)),
            pl.BlockSpec((L, LANES), lambda b: (b, 0)),
            pl.BlockSpec((1, SUBLANES, C_XBC), lambda b: (b, 0, 0)),
            pl.BlockSpec((1, C_GROUPS, C_STATE, C_GROUP_W), lambda b: (b, 0, 0, 0)),
            full((C_CONV, C_INNER)),
            full((C_CONV, nbc)),
            full((1, C_INNER)),
            full((1, nbc)),
            full((LANES, C_INNER)),
            full((SUBLANES, LANES)),
            full((SUBLANES, C_INNER)),
        ],
        out_specs=[
            pl.BlockSpec((L, C_INNER), lambda b: (b, 0)),
            pl.BlockSpec((1, C_CONV - 1, C_XBC), lambda b: (b, 0, 0)),
            pl.BlockSpec((1, C_GROUPS, C_STATE, C_GROUP_W), lambda b: (b, 0, 0, 0)),
        ],
        out_shape=[
            SDS((T, C_INNER), BF16),
            SDS((bsz, C_CONV - 1, C_XBC), F32),
            SDS((bsz, C_GROUPS, C_STATE, C_GROUP_W), F32),
        ],
        scratch_shapes=[pltpu.VMEM((C_GROUPS, C_STATE, C_GROUP_W), F32)],
        compiler_params=_cparams(("parallel",)),
        name="ssd",
    )(z, z, z, zs, cst, sst, cw[:, :C_INNER], cw[:, C_INNER:], cb[:, :C_INNER], cb[:, C_INNER:], expand, prm, rows)


def _merge_kernel(x_ref, ya_ref, yb_ref, yc_ref, ga_ref, gb_ref, gc_ref, wa_ref, wb_ref, wc_ref, wo_ref, o_ref):
    m = jax.nn.sigmoid(ga_ref[...].astype(F32)) * _dot(ya_ref[...], wa_ref[...])
    m = m + jax.nn.sigmoid(gb_ref[...].astype(F32)) * _dot(yb_ref[...], wb_ref[...])
    m = m + jax.nn.sigmoid(gc_ref[...].astype(F32)) * _dot(yc_ref[...], wc_ref[...])
    o_ref[...] = x_ref[...] + _dot(m.astype(BF16), wo_ref[...])


def _merge(x2d, ya, yb, yc, z, wa, wb, wc, wo):
    T = x2d.shape[0]
    tm = min(512, T)
    row = lambda w, cb=0: pl.BlockSpec((tm, w), lambda i: (i, cb))
    full = lambda shape: pl.BlockSpec(shape, lambda i: (0, 0))
    return pl.pallas_call(
        _merge_kernel,
        grid=(T // tm,),
        in_specs=[
            row(D_MODEL), row(A_WIDTH), row(B_WIDTH), row(C_INNER),
            row(D_MODEL, ZB_GATES), row(D_MODEL, ZB_GATES + 1), row(D_MODEL, ZB_GATES + 2),
            full((A_WIDTH, D_MODEL)), full((B_WIDTH, D_MODEL)), full((C_INNER, D_MODEL)), full((D_MODEL, D_MODEL)),
        ],
        out_specs=row(D_MODEL),
        out_shape=SDS((T, D_MODEL), F32),
        compiler_params=_cparams(("parallel",)),
        name="merge",
    )(x2d, ya, yb, yc, z, z, z, wa, wb, wc, wo)


def _memkv_kernel(m_ref, g_ref, wk_ref, wv_ref, k_ref, v_ref):
    hm = _rms(m_ref[0], g_ref[...]).astype(BF16)
    k_ref[0] = _dot(hm, wk_ref[...])
    v_ref[0] = _dot(hm, wv_ref[...])


def _memory_kv(mem, g, wk, wv):
    bsz = mem.shape[0]
    blk = pl.BlockSpec((1, N_MEM, D_MODEL), lambda b: (b, 0, 0))
    full = lambda shape: pl.BlockSpec(shape, lambda b: (0, 0))
    return pl.pallas_call(
        _memkv_kernel,
        grid=(bsz,),
        in_specs=[blk, full((1, D_MODEL)), full((D_MODEL, D_MODEL)), full((D_MODEL, D_MODEL))],
        out_specs=[blk, blk],
        out_shape=[SDS((bsz, N_MEM, D_MODEL), F32)] * 2,
        compiler_params=_cparams(("parallel",)),
        name="memory_kv",
    )(mem, g, wk, wv)


def _xattn_kernel(x_ref, mk_ref, mv_ref, g_ref, wq_ref, wo_ref, o_ref, mk16, mv16):
    @pl.when(pl.program_id(1) == 0)
    def _():
        mk16[...] = mk_ref[0].astype(BF16)
        mv16[...] = mv_ref[0].astype(BF16)

    x = x_ref[...]
    hq = _rms(x, g_ref[...]).astype(BF16)
    q = (_dot(hq, wq_ref[...]) * (X_HEAD_DIM ** -0.5)).astype(BF16)
    H = range(X_HEADS)
    hs = [slice(h * X_HEAD_DIM, (h + 1) * X_HEAD_DIM) for h in H]
    s = [_dot_nt(q[:, hs[h]], mk16[:, hs[h]]) for h in H]
    p = [jnp.exp(s[h] - jnp.max(s[h], axis=-1, keepdims=True)) for h in H]
    l = [jnp.sum(p[h], axis=-1, keepdims=True) for h in H]
    o = [(_dot(p[h].astype(BF16), mv16[:, hs[h]]) / l[h]).astype(BF16) for h in H]
    acc = x
    for h in H:
        acc = acc + _dot(o[h], wo_ref[hs[h], :])
    o_ref[...] = acc


def _xattn(x2d, bsz, L, mk, mv, g, wq, wo):
    tq = min(512, L)
    nq = L // tq
    row = pl.BlockSpec((tq, D_MODEL), lambda b, t: (b * nq + t, 0))
    mem = pl.BlockSpec((1, N_MEM, D_MODEL), lambda b, t: (b, 0, 0))
    full = lambda shape: pl.BlockSpec(shape, lambda b, t: (0, 0))
    return pl.pallas_call(
        _xattn_kernel,
        grid=(bsz, nq),
        in_specs=[row, mem, mem, full((1, D_MODEL)), full((D_MODEL, D_MODEL)), full((D_MODEL, D_MODEL))],
        out_specs=row,
        out_shape=SDS((bsz * L, D_MODEL), F32),
        scratch_shapes=[pltpu.VMEM((N_MEM, D_MODEL), BF16)] * 2,
        compiler_params=_cparams(("parallel", "arbitrary")),
        name="xattn",
    )(x2d, mk, mv, g, wq, wo)


def _ffn_kernel(*refs, final_norm):
    if final_norm:
        x_ref, g_ref, wu_ref, wd_ref, cw_ref, cb_ref, fst_ref, gf_ref, o_ref, fst_o_ref, halo_scr = refs
    else:
        x_ref, g_ref, wu_ref, wd_ref, cw_ref, cb_ref, fst_ref, o_ref, fst_o_ref, halo_scr = refs

    @pl.when(pl.program_id(1) == 0)
    def _():
        halo_scr[...] = fst_ref[0]

    x = x_ref[...]
    hf = _rms(x, g_ref[...]).astype(BF16)
    acc = None
    for c in range(D_FF // FFN_CHUNK):
        cs = slice(c * FFN_CHUNK, (c + 1) * FFN_CHUNK)
        u = _dot(hf, wu_ref[:, cs])
        gpre = _dot(hf, wu_ref[:, D_FF + c * FFN_CHUNK:D_FF + (c + 1) * FFN_CHUNK])
        conv, new_halo = _causal_conv(halo_scr[:, cs], gpre, cw_ref[:, cs], F_CONV)
        halo_scr[:, cs] = new_halo
        act = (u * _silu(conv + cb_ref[:, cs])).astype(BF16)
        contrib = _dot(act, wd_ref[cs, :])
        acc = contrib if acc is None else acc + contrib
    fst_o_ref[0, 0] = halo_scr[...]
    y = x + acc
    o_ref[...] = _rms(y, gf_ref[...]) if final_norm else y


def _ffn(x2d, bsz, L, g, wup, wdn, cw, cb, fst, gfinal):
    tr = min(FFN_ROWS_PER_STEP, L)
    nt = L // tr
    assert L % tr == 0 and D_FF % FFN_CHUNK == 0
    final_norm = gfinal is not None
    row = pl.BlockSpec((tr, D_MODEL), lambda b, t: (b * nt + t, 0))
    const = lambda shape: pl.BlockSpec(shape, lambda b, t: (0, 0), pipeline_mode=pl.Buffered(1))
    in_specs = [
        row, const((1, D_MODEL)), const((D_MODEL, 2 * D_FF)), const((D_FF, D_MODEL)),
        const((F_CONV, D_FF)), const((1, D_FF)),
        pl.BlockSpec((1, SUBLANES, D_FF), lambda b, t: (b, 0, 0)),
    ]
    args = [x2d, g, wup, wdn, cw, cb, fst]
    if final_norm:
        in_specs.append(const((1, D_MODEL)))
        args.append(gfinal)
    return pl.pallas_call(
        functools.partial(_ffn_kernel, final_norm=final_norm),
        grid=(bsz, nt),
        in_specs=in_specs,
        out_specs=[row, pl.BlockSpec((1, 1, SUBLANES, D_FF), lambda b, t: (b, t, 0, 0))],
        out_shape=[SDS((bsz * L, D_MODEL), F32), SDS((bsz, nt, SUBLANES, D_FF), F32)],
        scratch_shapes=[pltpu.VMEM((SUBLANES, D_FF), F32)],
        compiler_params=_cparams(("parallel", "arbitrary")),
        name="ffn",
    )(*args)


def _pad_state(st):
    return jnp.pad(st.astype(F32), ((0, 0), (SUBLANES - st.shape[1], 0), (0, 0)))


def _head_row(v, width):
    return jnp.repeat(v.astype(F32), width)[None, :]


def _prep_layer(l, p):
    w_in = p['w_in'][l]
    w_main = jnp.concatenate([w_in[:, a:b] for a, b in _SEGS_MAIN], axis=1).astype(BF16)
    n_small = sum(b - a for a, b in _SEGS_SMALL)
    w_small = jnp.concatenate([w_in[:, a:b] for a, b in _SEGS_SMALL] + [jnp.zeros((D_MODEL, LANES - n_small), F32)],
                              axis=1).astype(BF16)
    prm = jnp.zeros((SUBLANES, LANES), F32)
    prm = prm.at[0, SM_DEC:SM_DEC + B_HEADS].set(p['b_dt_bias'][l])
    prm = prm.at[0, SM_DT:SM_DT + C_HEADS].set(p['c_dt_bias'][l])
    prm = prm.at[1, SM_DEC:SM_DEC + B_HEADS].set(p['b_a_log'][l])
    prm = prm.at[1, SM_DT:SM_DT + C_HEADS].set(p['c_a_log'][l])
    rows = jnp.zeros((SUBLANES, C_INNER), F32)
    rows = rows.at[0].set(_head_row(p['c_dt_bias'][l], C_HEAD_DIM)[0])
    rows = rows.at[1].set(_head_row(p['c_a_log'][l], C_HEAD_DIM)[0])
    rows = rows.at[2].set(_head_row(p['c_d'][l], C_HEAD_DIM)[0])
    rows = rows.at[3].set(p['c_norm'][l].astype(F32))
    return dict(
        norm_mix=p['norm_mix'][l][None, :], w_main=w_main, w_small=w_small,
        rel=p['a_rel_bias'][l],
        b_conv_w=p['b_conv_w'][l], prm=prm, b_norm=p['b_norm'][l][None, :],
        c_conv_w=p['c_conv_w'][l], c_conv_b=p['c_conv_b'][l][None, :], rows=rows,
        wa=p['w_br_a'][l].astype(BF16), wb=p['w_br_b'][l].astype(BF16), wc=p['w_br_c'][l].astype(BF16),
        wo=p['w_out'][l].astype(BF16),
        norm_x=p['norm_x'][l][None, :], norm_mem=p['norm_mem'][l][None, :],
        wxq=p['wx_q'][l].astype(BF16), wxk=p['wx_k'][l].astype(BF16), wxv=p['wx_v'][l].astype(BF16),
        wxo=p['wx_o'][l].astype(BF16),
        norm_ffn=p['norm_ffn'][l][None, :], w_up=p['w_up'][l].astype(BF16), w_down=p['w_down'][l].astype(BF16),
        f_conv_w=p['f_conv_w'][l], f_conv_b=p['f_conv_b'][l][None, :],
    )


def _expand_matrix():
    e = np.zeros((LANES, C_INNER), np.float32)
    for h in range(C_HEADS):
        e[SM_DT + h, h * C_HEAD_DIM:(h + 1) * C_HEAD_DIM] = 1.0
    return jnp.asarray(e)


def _layer(x2d, bsz, L, lw, a_past, b_conv, b_rec, c_conv, c_ssm, f_conv, mk, mv, gfinal, expand):
    z, zs = _in_proj(x2d, lw['norm_mix'], lw['w_main'], lw['w_small'])

    C = min(CHUNK, L)
    G = 2 if L // C >= 2 else 1
    bias = _attn_bias(lw['rel'], C, G)
    past = None
    if a_past is not None:
        past = tuple(t.reshape(bsz, A_WINDOW, A_WIDTH) for t in a_past)
    ya = _band_attn(z, bsz, L, C, G, bias, past)
    keep = min(A_WINDOW, L)
    z3 = z.reshape(bsz, L, NZ)
    a_k = z3[:, L - keep:, A_WIDTH:2 * A_WIDTH].astype(F32).reshape(bsz, keep, A_HEADS, A_HEAD_DIM)
    a_v = z3[:, L - keep:, 2 * A_WIDTH:3 * A_WIDTH].astype(F32).reshape(bsz, keep, A_HEADS, A_HEAD_DIM)

    yb, b_conv_new, b_rec_new = _gdn(z, zs, bsz, L, _pad_state(b_conv), b_rec.astype(F32),
                                     lw['b_conv_w'], lw['prm'], lw['b_norm'])

    sst = jnp.transpose(c_ssm.astype(F32).reshape(bsz, C_GROUPS, C_HEADS // C_GROUPS, C_HEAD_DIM, C_STATE),
                        (0, 1, 4, 2, 3)).reshape(bsz, C_GROUPS, C_STATE, C_GROUP_W)
    yc, c_conv_new, sst_new = _ssd(z, zs, bsz, L, _pad_state(c_conv), sst, lw['c_conv_w'], lw['c_conv_b'],
                                   expand, lw['prm'], lw['rows'])
    c_ssm_new = jnp.transpose(sst_new.reshape(bsz, C_GROUPS, C_STATE, C_HEADS // C_GROUPS, C_HEAD_DIM),
                              (0, 1, 3, 4, 2)).reshape(bsz, C_HEADS, C_HEAD_DIM, C_STATE)

    x2d = _merge(x2d, ya, yb, yc, z, lw['wa'], lw['wb'], lw['wc'], lw['wo'])
    x2d = _xattn(x2d, bsz, L, mk, mv, lw['norm_x'], lw['wxq'], lw['wxo'])

    x2d, fst_new = _ffn(x2d, bsz, L, lw['norm_ffn'], lw['w_up'], lw['w_down'], lw['f_conv_w'], lw['f_conv_b'],
                        _pad_state(f_conv), gfinal)
    f_conv_new = fst_new[:, -1, SUBLANES - (F_CONV - 1):, :]
    return x2d, (a_k, a_v, b_conv_new, b_rec_new, c_conv_new, c_ssm_new, f_conv_new)


def kernel(x_prompt, x_sample, cache_attn_k, cache_attn_v, state_b_conv, state_b_rec, state_c_conv, state_c_ssm, state_ffn_conv, cache_mem_k, cache_mem_v, mem_prompt, norm_mix, w_in, a_rel_bias, b_conv_w, b_a_log, b_dt_bias, b_norm, c_conv_w, c_conv_b, c_dt_bias, c_a_log, c_d, c_norm, w_br_a, w_br_b, w_br_c, w_out, norm_x, norm_mem, wx_q, wx_k, wx_v, wx_o, norm_ffn, w_up, f_conv_w, f_conv_b, w_down, norm_final):
    params = dict(norm_mix=norm_mix, w_in=w_in, a_rel_bias=a_rel_bias, b_conv_w=b_conv_w, b_a_log=b_a_log,
                  b_dt_bias=b_dt_bias, b_norm=b_norm, c_conv_w=c_conv_w, c_conv_b=c_conv_b, c_dt_bias=c_dt_bias,
                  c_a_log=c_a_log, c_d=c_d, c_norm=c_norm, w_br_a=w_br_a, w_br_b=w_br_b, w_br_c=w_br_c,
                  w_out=w_out, norm_x=norm_x, norm_mem=norm_mem, wx_q=wx_q, wx_k=wx_k, wx_v=wx_v, wx_o=wx_o,
                  norm_ffn=norm_ffn, w_up=w_up, f_conv_w=f_conv_w, f_conv_b=f_conv_b, w_down=w_down)
    nb, seq, _ = x_prompt.shape
    db, dseq, _ = x_sample.shape
    expand = _expand_matrix()
    gfin = norm_final[None, :]
    xp = x_prompt.reshape(nb * seq, D_MODEL)
    xs = x_sample.reshape(db * dseq, D_MODEL)
    p_states, s_states, p_mk, p_mv = [], [], [], []
    for l in range(DEPTH):
        lw = _prep_layer(l, params)
        last = gfin if l == DEPTH - 1 else None
        mk, mv = _memory_kv(mem_prompt, lw['norm_mem'], lw['wxk'], lw['wxv'])
        xp, st_p = _layer(
            xp, nb, seq, lw, None,
            jnp.zeros((nb, B_CONV - 1, 3 * B_WIDTH), F32),
            jnp.zeros((nb, B_HEADS, B_HEAD_DIM, B_HEAD_DIM), F32),
            jnp.zeros((nb, C_CONV - 1, C_XBC), F32),
            jnp.zeros((nb, C_HEADS, C_HEAD_DIM, C_STATE), F32),
            jnp.zeros((nb, F_CONV - 1, D_FF), F32),
            mk, mv, last, expand)
        p_states.append(st_p)
        p_mk.append(mk.reshape(nb, N_MEM, X_HEADS, X_HEAD_DIM))
        p_mv.append(mv.reshape(nb, N_MEM, X_HEADS, X_HEAD_DIM))
        xs, st_s = _layer(
            xs, db, dseq, lw, (cache_attn_k[l], cache_attn_v[l]),
            state_b_conv[l], state_b_rec[l], state_c_conv[l], state_c_ssm[l], state_ffn_conv[l],
            cache_mem_k[l].reshape(db, N_MEM, D_MODEL), cache_mem_v[l].reshape(db, N_MEM, D_MODEL),
            last, expand)
        s_states.append(st_s)

    y_prompt = xp.reshape(nb, seq, D_MODEL)
    y_sample = xs.reshape(db, dseq, D_MODEL)
    pst = [jnp.stack([s[i] for s in p_states]) for i in range(7)]
    sst = [jnp.stack([s[i] for s in s_states]) for i in range(7)]
    return (y_prompt, y_sample, pst[0], pst[1], pst[2], pst[3], pst[4], pst[5], pst[6],
            jnp.stack(p_mk), jnp.stack(p_mv),
            sst[0], sst[1], sst[2], sst[3], sst[4], sst[5], sst[6])
```

```python
import functools

import jax
import jax.numpy as jnp
import numpy as np
from jax import lax
from jax.experimental import pallas as pl
from jax.experimental.pallas import tpu as pltpu

F32 = jnp.float32
BF16 = jnp.bfloat16
SDS = jax.ShapeDtypeStruct

D_MODEL = 1024
DEPTH = 2
CHUNK = 64
N_MEM = 256
EPS = 1e-6

A_HEADS = 8
A_HEAD_DIM = 64
A_WIDTH = A_HEADS * A_HEAD_DIM
A_WINDOW = 8 * CHUNK
A_MAX_REL = 128

B_HEADS = 4
B_HEAD_DIM = 128
B_WIDTH = B_HEADS * B_HEAD_DIM
B_CONV = 4

C_HEADS = 16
C_HEAD_DIM = 64
C_INNER = C_HEADS * C_HEAD_DIM
C_GROUPS = 2
C_STATE = 128
C_XBC = C_INNER + 2 * C_GROUPS * C_STATE
C_CONV = 4
C_GROUP_W = C_INNER // C_GROUPS

X_HEADS = 4
X_HEAD_DIM = D_MODEL // X_HEADS

D_FF = 2816
F_CONV = 3

LANES = 128
SUBLANES = 8
NEG = -1e30
FFN_ROWS_PER_STEP = 1024
FFN_CHUNK = 256
GDN_SEQS_PER_STEP = 4
GDN_ROWS_PER_STEP = 512
ATTN_HEADS_PER_STEP = 4
VMEM_LIMIT = 56 * 1024 * 1024

NZ = 9216
ZW_A = A_WIDTH
ZB_BQKV = 1
ZB_CZ = 3
ZB_CX = 4
ZB_CBC = 10
ZB_BGATE = 11
ZB_GATES = 6
SM_BETA = 0
SM_DEC = 4
SM_DT = 8

_O_BQKV = 3 * A_WIDTH
_O_BETA = _O_BQKV + 3 * B_WIDTH
_O_DEC = _O_BETA + B_HEADS
_O_BGATE = _O_DEC + B_HEADS
_O_CZ = _O_BGATE + B_WIDTH
_O_CXBC = _O_CZ + C_INNER
_O_CDT = _O_CXBC + C_XBC
_O_GATES = _O_CDT + C_HEADS
_SEGS_MAIN = ((0, _O_BETA), (_O_CZ, _O_CXBC), (_O_CXBC, _O_CDT), (_O_BGATE, _O_CZ), (_O_GATES, _O_GATES + 3 * D_MODEL))
_SEGS_SMALL = ((_O_BETA, _O_BGATE), (_O_CDT, _O_GATES))


def _cparams(sem):
    return pltpu.CompilerParams(dimension_semantics=sem, vmem_limit_bytes=VMEM_LIMIT)


def _rms(x, g):
    return x * lax.rsqrt(jnp.mean(x * x, axis=-1, keepdims=True) + EPS) * g


def _silu(x):
    return x * jax.nn.sigmoid(x)


def _softplus(x):
    return jnp.maximum(x, 0.0) + jnp.log1p(jnp.exp(-jnp.abs(x)))


def _dot(a, b):
    return jnp.dot(a, b, preferred_element_type=F32)


def _pieces(a, n):
    out = []
    for _ in range(n - 1):
        p = a.astype(BF16)
        out.append(p)
        a = a - p.astype(F32)
    out.append(a.astype(BF16))
    return out


def _dot_sel(sel, b):
    s16 = sel.astype(BF16)
    b1, b2, b3 = _pieces(b, 3)
    return _dot(s16, b1) + (_dot(s16, b2) + _dot(s16, b3))


def _dot_pick(a, sel):
    s16 = sel.astype(BF16)
    a1, a2, a3 = _pieces(a, 3)
    return _dot(a1, s16) + (_dot(a2, s16) + _dot(a3, s16))


def _dot_x3(a, b):
    ah, al = _pieces(a, 2)
    bh, bl = _pieces(b, 2)
    return _dot(ah, bh) + (_dot(ah, bl) + _dot(al, bh))


def _dot_nt(a, b):
    return lax.dot_general(a, b, (((1,), (1,)), ((), ())), preferred_element_type=F32)


def _dot_tn(a, b):
    return lax.dot_general(a, b, (((0,), (0,)), ((), ())), preferred_element_type=F32)


def _row_start(i, n):
    return i * n if isinstance(i, int) else pl.multiple_of(i * n, n)


def _causal_conv(prev8, cur, w_ref, taps):
    ext = jnp.concatenate([prev8, cur], axis=0)
    acc = cur * w_ref[taps - 1:taps, :]
    for s in range(1, taps):
        acc = acc + pltpu.roll(ext, s, 0)[SUBLANES:, :] * w_ref[taps - 1 - s:taps - s, :]
    return acc, ext[cur.shape[0]:, :]


def _inproj_kernel(x_ref, g_ref, w_ref, ws_ref, z_ref, zs_ref, h_scr):
    @pl.when(pl.program_id(1) == 0)
    def _():
        hb = _rms(x_ref[...], g_ref[...]).astype(BF16)
        h_scr[...] = hb
        zs_ref[...] = _dot(hb, ws_ref[...])

    z_ref[...] = _dot(h_scr[...], w_ref[...]).astype(BF16)


def _in_proj(x2d, g, w_main, w_small):
    T = x2d.shape[0]
    tm = min(1024, T)
    tn = 1536
    return pl.pallas_call(
        _inproj_kernel,
        grid=(T // tm, NZ // tn),
        in_specs=[
            pl.BlockSpec((tm, D_MODEL), lambda i, j: (i, 0)),
            pl.BlockSpec((1, D_MODEL), lambda i, j: (0, 0)),
            pl.BlockSpec((D_MODEL, tn), lambda i, j: (0, j)),
            pl.BlockSpec((D_MODEL, LANES), lambda i, j: (0, 0)),
        ],
        out_specs=[
            pl.BlockSpec((tm, tn), lambda i, j: (i, j)),
            pl.BlockSpec((tm, LANES), lambda i, j: (i, 0)),
        ],
        out_shape=[SDS((T, NZ), BF16), SDS((T, LANES), F32)],
        scratch_shapes=[pltpu.VMEM((tm, D_MODEL), BF16)],
        compiler_params=_cparams(("parallel", "arbitrary")),
        name="in_proj",
    )(x2d, g, w_main, w_small)


def _attn_kernel(*refs, L, C, G, has_past):
    if has_past:
        q_ref, k_ref, v_ref, pk_ref, pv_ref, bias_ref, o_ref, kx, vx = refs
        kx[0:A_WINDOW, :] = pk_ref[0].astype(BF16)
        vx[0:A_WINDOW, :] = pv_ref[0].astype(BF16)
    else:
        q_ref, k_ref, v_ref, bias_ref, o_ref, kx, vx = refs
        kx[0:A_WINDOW, :] = jnp.zeros((A_WINDOW, A_WIDTH), BF16)
        vx[0:A_WINDOW, :] = jnp.zeros((A_WINDOW, A_WIDTH), BF16)
    kx[A_WINDOW:A_WINDOW + L, :] = k_ref[...]
    vx[A_WINDOW:A_WINDOW + L, :] = v_ref[...]

    GC = G * C
    NB = A_WINDOW + GC
    lane = lax.broadcasted_iota(jnp.int32, (1, LANES), 1)
    first_head = lane < A_HEAD_DIM
    col = lax.broadcasted_iota(jnp.int32, (1, NB), 1)
    scale = A_HEAD_DIM ** -0.5

    def group(g, carry, masked):
        r0 = _row_start(g, GC)
        q = q_ref[pl.ds(r0, GC), :] * jnp.asarray(scale, BF16)
        kb = kx[pl.ds(r0, NB), :]
        vb = vx[pl.ds(r0, NB), :]
        for h0 in range(0, A_HEADS, ATTN_HEADS_PER_STEP):
            hs = range(h0, h0 + ATTN_HEADS_PER_STEP)
            sl = {h: slice((h // 2) * LANES, (h // 2 + 1) * LANES) for h in hs}
            msk = {h: first_head if h % 2 == 0 else jnp.logical_not(first_head) for h in hs}
            s = {h: _dot_nt(jnp.where(msk[h], q[:, sl[h]], jnp.zeros((GC, LANES), BF16)), kb[:, sl[h]]) for h in hs}
            s = {h: s[h] + bias_ref[h] for h in hs}
            if masked:
                s = {h: jnp.where(r0 + col >= A_WINDOW, s[h], NEG) for h in hs}
            p = {h: jnp.exp(s[h] - jnp.max(s[h], axis=-1, keepdims=True)) for h in hs}
            l = {h: jnp.sum(p[h], axis=-1, keepdims=True) for h in hs}
            pv = {h: _dot(p[h].astype(BF16), vb[:, sl[h]]) / l[h] for h in hs}
            for h in hs:
                if h % 2 == 1:
                    o_ref[pl.ds(r0, GC), sl[h]] = jnp.where(first_head, pv[h - 1], pv[h]).astype(BF16)
        return carry

    ng = L // GC
    n_masked = 0 if has_past else min(ng, -(-A_WINDOW // GC))
    if ng == 1:
        group(0, 0, n_masked > 0)
    else:
        if n_masked:
            lax.fori_loop(0, n_masked, functools.partial(group, masked=True), 0)
        if ng > n_masked:
            lax.fori_loop(n_masked, ng, functools.partial(group, masked=False), 0)


def _attn_bias(table, C, G):
    GC = G * C
    NB = A_WINDOW + GC
    r = np.arange(GC)[:, None]
    j = np.arange(NB)[None, :]
    lo = (r // C) * C
    allowed = (j >= lo) & (j < lo + A_WINDOW + C)
    dmax = A_WINDOW + GC - 1
    rel = np.clip(dmax - np.arange(NB + GC - 1), -A_MAX_REL, A_MAX_REL) + A_MAX_REL
    vflip = table[rel].astype(F32).T
    starts = jnp.asarray(GC - 1 - np.arange(GC), jnp.int32)
    b = jax.vmap(lambda st: lax.dynamic_slice_in_dim(vflip, st, NB, axis=1), out_axes=1)(starts)
    return jnp.where(allowed[None], b, NEG)


def _band_attn(z, bsz, L, C, G, bias, past):
    T = bsz * L
    has_past = past is not None
    NB = A_WINDOW + G * C
    in_specs = [pl.BlockSpec((L, A_WIDTH), lambda b, i=i: (b, i)) for i in range(3)]
    args = [z, z, z]
    if has_past:
        in_specs += [pl.BlockSpec((1, A_WINDOW, A_WIDTH), lambda b: (b, 0, 0))] * 2
        args += list(past)
    in_specs.append(pl.BlockSpec((A_HEADS, G * C, NB), lambda b: (0, 0, 0)))
    args.append(bias)
    return pl.pallas_call(
        functools.partial(_attn_kernel, L=L, C=C, G=G, has_past=has_past),
        grid=(bsz,),
        in_specs=in_specs,
        out_specs=pl.BlockSpec((L, A_WIDTH), lambda b: (b, 0)),
        out_shape=SDS((T, A_WIDTH), BF16),
        scratch_shapes=[pltpu.VMEM((A_WINDOW + L, A_WIDTH), BF16)] * 2,
        compiler_params=_cparams(("parallel",)),
        name="band_attn",
    )(*args)


def _gdn_kernel(qkv_ref, sm_ref, gate_ref, cst_ref, rst_ref, cw_ref, prm_ref, bn_ref,
                yb_ref, cst_o_ref, rst_o_ref, s_scr, prev_scr, *, nb, lb, cl):
    t = pl.program_id(1)
    nc = lb // cl

    @pl.when(t == 0)
    def _():
        for n in range(nb):
            for h in range(B_HEADS):
                s_scr[n * B_HEADS + h] = rst_ref[n, h]
        prev_scr[...] = cst_ref[...]

    bias_row = prm_ref[0:1, :]
    aneg_row = -jnp.exp(prm_ref[1:2, :])
    bn = bn_ref[...]
    tril = (lax.broadcasted_iota(jnp.int32, (cl, cl), 0) >= lax.broadcasted_iota(jnp.int32, (cl, cl), 1)).astype(F32)
    ri = lax.broadcasted_iota(jnp.int32, (cl, 2 * cl), 0)
    lane2 = lax.broadcasted_iota(jnp.int32, (cl, 2 * cl), 1)
    ci = lane2 & (cl - 1)
    first = lane2 < cl
    incl = ri >= ci
    strict = ri > ci
    eye = (ri == ci).astype(F32)
    sh = int(np.log2(SUBLANES))
    diag_blk = (ri >> sh) == (ci >> sh)
    sub_blk = []
    while (1 << sh) < cl:
        sub_blk.append(((ri >> (sh + 1)) == (ci >> (sh + 1))) & (((ri >> sh) & 1) == 1) & (((ci >> sh) & 1) == 0))
        sh += 1
    U = [(n, h) for n in range(nb) for h in range(B_HEADS)]
    PR = [(U[i], U[i + 1]) for i in range(0, len(U), 2)]
    zk = jnp.zeros((cl, B_HEAD_DIM), BF16)
    zr = jnp.zeros((cl, 2 * B_HEAD_DIM), BF16)

    def bdiag(x):
        z = jnp.zeros_like(x)
        return jnp.concatenate([jnp.where(first, x, z), jnp.where(first, z, x)], axis=0)

    def split_bd(x):
        hi, lo = _pieces(x, 2)
        return bdiag(hi), bdiag(lo)

    def x3(a, bd_hi, bd_lo):
        ah, al = _pieces(a, 2)
        both = _dot(jnp.concatenate([ah, al], axis=0), bd_hi)
        return both[:cl] + (both[cl:] + _dot(ah, bd_lo))

    def body(c, carry):
        r0 = _row_start(c, cl)
        rows = pl.ds(r0, cl)
        act, beta_all, gv = [], [], []
        for n in range(nb):
            conv, new_prev = _causal_conv(prev_scr[n], qkv_ref[n, rows, :].astype(F32), cw_ref, B_CONV)
            prev_scr[n] = new_prev
            act.append(_silu(conv))
            smc = sm_ref[n, rows, :]
            beta_all.append(jax.nn.sigmoid(smc))
            gv.append(_softplus(smc + bias_row) * aneg_row)
        gcs = [_dot_sel(tril, gv[n]) for n in range(nb)]
        q = {(n, h): act[n][:, h * B_HEAD_DIM:(h + 1) * B_HEAD_DIM] for n, h in U}
        k = {(n, h): act[n][:, B_WIDTH + h * B_HEAD_DIM:B_WIDTH + (h + 1) * B_HEAD_DIM] for n, h in U}
        v = {(n, h): act[n][:, 2 * B_WIDTH + h * B_HEAD_DIM:2 * B_WIDTH + (h + 1) * B_HEAD_DIM] for n, h in U}
        q = {u: q[u] * lax.rsqrt(jnp.sum(q[u] * q[u], axis=-1, keepdims=True) + EPS) * (B_HEAD_DIM ** -0.5) for u in U}
        k = {u: k[u] * lax.rsqrt(jnp.sum(k[u] * k[u], axis=-1, keepdims=True) + EPS) for u in U}
        beta = {(n, h): beta_all[n][:, SM_BETA + h:SM_BETA + h + 1] for n, h in U}
        g = {(n, h): gv[n][:, SM_DEC + h:SM_DEC + h + 1] for n, h in U}
        gc = {(n, h): gcs[n][:, SM_DEC + h:SM_DEC + h + 1] for n, h in U}
        gl = {(n, h): gcs[n][cl - 1:cl, SM_DEC + h:SM_DEC + h + 1] for n, h in U}
        e = {pr: _dot_sel(tril, jnp.where(strict, jnp.where(first, g[pr[0]], g[pr[1]]), 0.0)) for pr in PR}
        decay = {pr: jnp.where(incl, jnp.exp(e[pr]), 0.0) for pr in PR}
        kb = {u: k[u] * beta[u] for u in U}
        k16 = {u: k[u].astype(BF16) for u in U}
        kpad = {pr: (jnp.concatenate([k16[pr[0]], zk], axis=0), jnp.concatenate([zk, k16[pr[1]]], axis=0)) for pr in PR}
        a = {pr: _dot_nt(kb[pr[0]].astype(BF16), kpad[pr][0]) + _dot_nt(kb[pr[1]].astype(BF16), kpad[pr][1])
             for pr in PR}
        lm = {pr: jnp.where(strict, a[pr] * decay[pr], 0.0) for pr in PR}
        m = {pr: jnp.where(diag_blk, -lm[pr], 0.0) for pr in PR}
        p = {pr: eye + m[pr] for pr in PR}
        for _ in range(2):
            mb = {pr: split_bd(m[pr]) for pr in PR}
            m = {pr: x3(m[pr], *mb[pr]) for pr in PR}
            mb = {pr: split_bd(m[pr]) for pr in PR}
            p = {pr: p[pr] + x3(p[pr], *mb[pr]) for pr in PR}
        for cmask in sub_blk:
            pb = {pr: split_bd(p[pr]) for pr in PR}
            cx = {pr: x3(jnp.where(cmask, lm[pr], 0.0), *pb[pr]) for pr in PR}
            cb = {pr: split_bd(cx[pr]) for pr in PR}
            p = {pr: p[pr] - x3(p[pr], *cb[pr]) for pr in PR}
        egc = {u: jnp.exp(gc[u]) for u in U}
        rhs = {u: _pieces(jnp.concatenate([v[u] * beta[u], kb[u] * egc[u]], axis=1), 2) for u in U}
        rb = {pr: [jnp.concatenate([jnp.concatenate([rhs[pr[0]][i], zr], axis=1),
                                    jnp.concatenate([zr, rhs[pr[1]][i]], axis=1)], axis=0) for i in range(2)]
              for pr in PR}
        solp = {pr: x3(p[pr], *rb[pr]) for pr in PR}
        sol = {}
        for pr in PR:
            sol[pr[0]] = solp[pr][:, :2 * B_HEAD_DIM]
            sol[pr[1]] = solp[pr][:, 2 * B_HEAD_DIM:]
        qk = {pr: ((_dot_nt(q[pr[0]].astype(BF16), kpad[pr][0]) + _dot_nt(q[pr[1]].astype(BF16), kpad[pr][1]))
                   * decay[pr]).astype(BF16) for pr in PR}
        s0 = {(n, h): s_scr[n * B_HEADS + h] for n, h in U}
        s16 = {u: s0[u].astype(BF16) for u in U}
        ws = {u: _dot(jnp.concatenate([sol[u][:, B_HEAD_DIM:].astype(BF16), (q[u] * egc[u]).astype(BF16)], axis=0),
                      s16[u]) for u in U}
        uu = {u: sol[u][:, :B_HEAD_DIM] - ws[u][:cl] for u in U}
        u16 = {u: uu[u].astype(BF16) for u in U}
        kt = {u: (k[u] * jnp.exp(gl[u] - gc[u])).astype(BF16) for u in U}
        snew = {u: s0[u] * jnp.exp(gl[u]) + _dot_tn(kt[u], u16[u]) for u in U}
        o = {}
        for pr in PR:
            ustack = jnp.concatenate([u16[pr[0]], u16[pr[1]]], axis=0)
            zq = jnp.zeros_like(qk[pr])
            intra = _dot(jnp.concatenate([jnp.where(first, qk[pr], zq), jnp.where(first, zq, qk[pr])], axis=0), ustack)
            o[pr[0]] = ws[pr[0]][cl:] + intra[:cl]
            o[pr[1]] = ws[pr[1]][cl:] + intra[cl:]
        for n, h in U:
            hs = slice(h * B_HEAD_DIM, (h + 1) * B_HEAD_DIM)
            s_scr[n * B_HEADS + h] = snew[n, h]
            gate = gate_ref[n, rows, hs].astype(F32)
            yb_ref[n, rows, hs] = (_rms(o[n, h], bn) * _silu(gate)).astype(BF16)
        return carry

    if nc == 1:
        body(0, 0)
    else:
        lax.fori_loop(0, nc, body, 0)

    @pl.when(t == pl.num_programs(1) - 1)
    def _():
        cst_o_ref[...] = prev_scr[:, SUBLANES - (B_CONV - 1):, :]
        for n in range(nb):
            for h in range(B_HEADS):
                rst_o_ref[n, h] = s_scr[n * B_HEADS + h]


def _gdn(z, zs, bsz, L, cst, rst, cw, prm, bn):
    cl = min(CHUNK, L)
    nb = GDN_SEQS_PER_STEP
    lb = min(GDN_ROWS_PER_STEP, L)
    assert bsz % nb == 0 and L % lb == 0 and lb % cl == 0
    z3 = z.reshape(bsz, L, NZ)
    zs3 = zs.reshape(bsz, L, LANES)
    yb, cst_o, rst_o = pl.pallas_call(
        functools.partial(_gdn_kernel, nb=nb, lb=lb, cl=cl),
        grid=(bsz // nb, L // lb),
        in_specs=[
            pl.BlockSpec((nb, lb, 3 * B_WIDTH), lambda b, t: (b, t, ZB_BQKV)),
            pl.BlockSpec((nb, lb, LANES), lambda b, t: (b, t, 0)),
            pl.BlockSpec((nb, lb, B_WIDTH), lambda b, t: (b, t, ZB_BGATE)),
            pl.BlockSpec((nb, SUBLANES, 3 * B_WIDTH), lambda b, t: (b, 0, 0)),
            pl.BlockSpec((nb, B_HEADS, B_HEAD_DIM, B_HEAD_DIM), lambda b, t: (b, 0, 0, 0)),
            pl.BlockSpec((B_CONV, 3 * B_WIDTH), lambda b, t: (0, 0)),
            pl.BlockSpec((SUBLANES, LANES), lambda b, t: (0, 0)),
            pl.BlockSpec((1, B_HEAD_DIM), lambda b, t: (0, 0)),
        ],
        out_specs=[
            pl.BlockSpec((nb, lb, B_WIDTH), lambda b, t: (b, t, 0)),
            pl.BlockSpec((nb, B_CONV - 1, 3 * B_WIDTH), lambda b, t: (b, 0, 0)),
            pl.BlockSpec((nb, B_HEADS, B_HEAD_DIM, B_HEAD_DIM), lambda b, t: (b, 0, 0, 0)),
        ],
        out_shape=[
            SDS((bsz, L, B_WIDTH), BF16),
            SDS((bsz, B_CONV - 1, 3 * B_WIDTH), F32),
            SDS((bsz, B_HEADS, B_HEAD_DIM, B_HEAD_DIM), F32),
        ],
        scratch_shapes=[
            pltpu.VMEM((nb * B_HEADS, B_HEAD_DIM, B_HEAD_DIM), F32),
            pltpu.VMEM((nb, SUBLANES, 3 * B_WIDTH), F32),
        ],
        compiler_params=_cparams(("parallel", "arbitrary")),
        name="gdn",
    )(z3, zs3, z3, cst, rst, cw, prm, bn)
    return yb.reshape(bsz * L, B_WIDTH), cst_o, rst_o


def _ssd_kernel(cx_ref, cbc_ref, cz_ref, sm_ref, cst_ref, sst_ref, cwx_ref, cwbc_ref, cbx_ref, cbbc_ref,
                expand_ref, prm_ref, rows_ref, yc_ref, cst_o_ref, sst_o_ref, h_scr, *, L, cl):
    nc = L // cl
    h_scr[...] = sst_ref[0]
    bias_row = prm_ref[0:1, :]
    aneg_row = -jnp.exp(prm_ref[1:2, :])
    cdx = rows_ref[2:3, :]
    cn = rows_ref[3:4, :]
    ri = lax.broadcasted_iota(jnp.int32, (cl, cl), 0)
    ci = lax.broadcasted_iota(jnp.int32, (cl, cl), 1)
    tril = (ri >= ci).astype(F32)
    rx = lax.broadcasted_iota(jnp.int32, (cl, C_INNER), 0)
    jx = lax.broadcasted_iota(jnp.int32, (cl, C_INNER), 1) & (C_HEAD_DIM - 1)
    inclx = rx >= jx
    lane = lax.broadcasted_iota(jnp.int32, (1, LANES), 1)
    first_head = lane < C_HEAD_DIM

    def pad_rows(a):
        if cl == CHUNK:
            return a
        return jnp.concatenate([a, jnp.zeros((CHUNK - cl, a.shape[1]), a.dtype)], axis=0)

    def pad_lanes(a):
        if cl == C_HEAD_DIM:
            return a
        return jnp.concatenate([a, jnp.zeros((a.shape[0], C_HEAD_DIM - cl), a.dtype)], axis=1)

    def body(c, carry):
        px, pbc = carry
        r0 = _row_start(c, cl)
        convx, npx = _causal_conv(px, cx_ref[pl.ds(r0, cl), :].astype(F32), cwx_ref, C_CONV)
        convbc, npbc = _causal_conv(pbc, cbc_ref[pl.ds(r0, cl), :].astype(F32), cwbc_ref, C_CONV)
        xs = _silu(convx + cbx_ref[...])
        bcs = _silu(convbc + cbbc_ref[...])
        dt_c = _softplus(sm_ref[pl.ds(r0, cl), :] + bias_row)
        ac_c = _dot_sel(tril, dt_c * aneg_row)
        dtx = _dot_pick(dt_c, expand_ref[...])
        acx = _dot_pick(ac_c, expand_ref[...])
        ac_t = ac_c.T
        rowsj = [pad_lanes(ac_t[SM_DT + h:SM_DT + h + 1, :]) for h in range(C_HEADS)]
        ac_row = jnp.concatenate(rowsj, axis=1)
        decayx = jnp.exp(jnp.where(inclx, acx - ac_row, NEG))
        alast = acx[cl - 1:cl, :]
        eac = jnp.exp(acx)
        ealast = jnp.exp(alast)
        xdt = xs * dtx
        xtil = (xdt * jnp.exp(alast - acx)).astype(BF16)
        ys = []
        for g in range(C_GROUPS):
            gs = slice(g * C_GROUP_W, (g + 1) * C_GROUP_W)
            bg = bcs[:, g * C_STATE:(g + 1) * C_STATE].astype(BF16)
            cg = bcs[:, (C_GROUPS + g) * C_STATE:(C_GROUPS + g + 1) * C_STATE].astype(BF16)
            brep = jnp.concatenate([pad_rows(bg)] * (C_GROUP_W // CHUNK), axis=0)
            w = (_dot_nt(cg, brep) * decayx[:, gs]).astype(BF16)
            hg = h_scr[g]
            yoff = _dot(cg, hg.astype(BF16)) * eac[:, gs]
            yd = []
            for pr in range(C_GROUP_W // LANES):
                lo = g * C_GROUP_W + pr * LANES
                xp = pad_rows(xdt[:, lo:lo + LANES])
                bd = jnp.concatenate([jnp.where(first_head, xp, 0.0), jnp.where(first_head, 0.0, xp)], axis=0)
                yd.append(_dot(w[:, pr * LANES:(pr + 1) * LANES], bd.astype(BF16)))
            h_scr[g] = hg * ealast[:, gs] + _dot_tn(bg, xtil[:, gs])
            ys.append(jnp.concatenate(yd, axis=1) + yoff)
        y = jnp.concatenate(ys, axis=1) + cdx * xs
        t = y * _silu(cz_ref[pl.ds(r0, cl), :].astype(F32))
        yc_ref[pl.ds(r0, cl), :] = _rms(t, cn).astype(BF16)
        return npx, npbc

    carry = (cst_ref[0, :, :C_INNER], cst_ref[0, :, C_INNER:])
    carry = body(0, carry) if nc == 1 else lax.fori_loop(0, nc, body, carry)
    cst_o_ref[0, :, :C_INNER] = carry[0][SUBLANES - (C_CONV - 1):, :]
    cst_o_ref[0, :, C_INNER:] = carry[1][SUBLANES - (C_CONV - 1):, :]
    sst_o_ref[0] = h_scr[...]


def _ssd(z, zs, bsz, L, cst, sst, cw, cb, expand, prm, rows):
    T = bsz * L
    cl = min(CHUNK, L)
    nbc = C_XBC - C_INNER
    full = lambda shape: pl.BlockSpec(shape, lambda b: (0,) * len(shape))
    return pl.pallas_call(
        functools.partial(_ssd_kernel, L=L, cl=cl),
        grid=(bsz,),
        in_specs=[
            pl.BlockSpec((L, C_INNER), lambda b: (b, ZB_CX)),
            pl.BlockSpec((L, nbc), lambda b: (b, ZB_CBC)),
            pl.BlockSpec((L, C_INNER), lambda b: (b, ZB_CZ)),
            pl.BlockSpec((L, LANES), lambda b: (b, 0)),
            pl.BlockSpec((1, SUBLANES, C_XBC), lambda b: (b, 0, 0)),
            pl.BlockSpec((1, C_GROUPS, C_STATE, C_GROUP_W), lambda b: (b, 0, 0, 0)),
            full((C_CONV, C_INNER)),
            full((C_CONV, nbc)),
            full((1, C_INNER)),
            full((1, nbc)),
            full((LANES, C_INNER)),
            full((SUBLANES, LANES)),
            full((SUBLANES, C_INNER)),
        ],
        out_specs=[
            pl.BlockSpec((L, C_INNER), lambda b: (b, 0)),
            pl.BlockSpec((1, C_CONV - 1, C_XBC), lambda b: (b, 0, 0)),
            pl.BlockSpec((1, C_GROUPS, C_STATE, C_GROUP_W), lambda b: (b, 0, 0, 0)),
        ],
        out_shape=[
            SDS((T, C_INNER), BF16),
            SDS((bsz, C_CONV - 1, C_XBC), F32),
            SDS((bsz, C_GROUPS, C_STATE, C_GROUP_W), F32),
        ],
        scratch_shapes=[pltpu.VMEM((C_GROUPS, C_STATE, C_GROUP_W), F32)],
        compiler_params=_cparams(("parallel",)),
        name="ssd",
    )(z, z, z, zs, cst, sst, cw[:, :C_INNER], cw[:, C_INNER:], cb[:, :C_INNER], cb[:, C_INNER:], expand, prm, rows)


def _merge_kernel(x_ref, ya_ref, yb_ref, yc_ref, ga_ref, gb_ref, gc_ref, wa_ref, wb_ref, wc_ref, wo_ref, o_ref):
    m = jax.nn.sigmoid(ga_ref[...].astype(F32)) * _dot(ya_ref[...], wa_ref[...])
    m = m + jax.nn.sigmoid(gb_ref[...].astype(F32)) * _dot(yb_ref[...], wb_ref[...])
    m = m + jax.nn.sigmoid(gc_ref[...].astype(F32)) * _dot(yc_ref[...], wc_ref[...])
    o_ref[...] = x_ref[...] + _dot(m.astype(BF16), wo_ref[...])


def _merge(x2d, ya, yb, yc, z, wa, wb, wc, wo):
    T = x2d.shape[0]
    tm = min(512, T)
    row = lambda w, cb=0: pl.BlockSpec((tm, w), lambda i: (i, cb))
    full = lambda shape: pl.BlockSpec(shape, lambda i: (0, 0))
    return pl.pallas_call(
        _merge_kernel,
        grid=(T // tm,),
        in_specs=[
            row(D_MODEL), row(A_WIDTH), row(B_WIDTH), row(C_INNER),
            row(D_MODEL, ZB_GATES), row(D_MODEL, ZB_GATES + 1), row(D_MODEL, ZB_GATES + 2),
            full((A_WIDTH, D_MODEL)), full((B_WIDTH, D_MODEL)), full((C_INNER, D_MODEL)), full((D_MODEL, D_MODEL)),
        ],
        out_specs=row(D_MODEL),
        out_shape=SDS((T, D_MODEL), F32),
        compiler_params=_cparams(("parallel",)),
        name="merge",
    )(x2d, ya, yb, yc, z, z, z, wa, wb, wc, wo)


def _memkv_kernel(m_ref, g_ref, wk_ref, wv_ref, k_ref, v_ref):
    hm = _rms(m_ref[0], g_ref[...]).astype(BF16)
    k_ref[0] = _dot(hm, wk_ref[...])
    v_ref[0] = _dot(hm, wv_ref[...])


def _memory_kv(mem, g, wk, wv):
    bsz = mem.shape[0]
    blk = pl.BlockSpec((1, N_MEM, D_MODEL), lambda b: (b, 0, 0))
    full = lambda shape: pl.BlockSpec(shape, lambda b: (0, 0))
    return pl.pallas_call(
        _memkv_kernel,
        grid=(bsz,),
        in_specs=[blk, full((1, D_MODEL)), full((D_MODEL, D_MODEL)), full((D_MODEL, D_MODEL))],
        out_specs=[blk, blk],
        out_shape=[SDS((bsz, N_MEM, D_MODEL), F32)] * 2,
        compiler_params=_cparams(("parallel",)),
        name="memory_kv",
    )(mem, g, wk, wv)


def _xattn_kernel(x_ref, mk_ref, mv_ref, g_ref, wq_ref, wo_ref, o_ref, mk16, mv16):
    @pl.when(pl.program_id(1) == 0)
    def _():
        mk16[...] = mk_ref[0].astype(BF16)
        mv16[...] = mv_ref[0].astype(BF16)

    x = x_ref[...]
    hq = _rms(x, g_ref[...]).astype(BF16)
    q = (_dot(hq, wq_ref[...]) * (X_HEAD_DIM ** -0.5)).astype(BF16)
    H = range(X_HEADS)
    hs = [slice(h * X_HEAD_DIM, (h + 1) * X_HEAD_DIM) for h in H]
    s = [_dot_nt(q[:, hs[h]], mk16[:, hs[h]]) for h in H]
    p = [jnp.exp(s[h] - jnp.max(s[h], axis=-1, keepdims=True)) for h in H]
    l = [jnp.sum(p[h], axis=-1, keepdims=True) for h in H]
    o = [(_dot(p[h].astype(BF16), mv16[:, hs[h]]) / l[h]).astype(BF16) for h in H]
    acc = x
    for h in H:
        acc = acc + _dot(o[h], wo_ref[hs[h], :])
    o_ref[...] = acc


def _xattn(x2d, bsz, L, mk, mv, g, wq, wo):
    tq = min(512, L)
    nq = L // tq
    row = pl.BlockSpec((tq, D_MODEL), lambda b, t: (b * nq + t, 0))
    mem = pl.BlockSpec((1, N_MEM, D_MODEL), lambda b, t: (b, 0, 0))
    full = lambda shape: pl.BlockSpec(shape, lambda b, t: (0, 0))
    return pl.pallas_call(
        _xattn_kernel,
        grid=(bsz, nq),
        in_specs=[row, mem, mem, full((1, D_MODEL)), full((D_MODEL, D_MODEL)), full((D_MODEL, D_MODEL))],
        out_specs=row,
        out_shape=SDS((bsz * L, D_MODEL), F32),
        scratch_shapes=[pltpu.VMEM((N_MEM, D_MODEL), BF16)] * 2,
        compiler_params=_cparams(("parallel", "arbitrary")),
        name="xattn",
    )(x2d, mk, mv, g, wq, wo)


def _ffn_kernel(*refs, final_norm):
    if final_norm:
        x_ref, g_ref, wu_ref, wd_ref, cw_ref, cb_ref, fst_ref, gf_ref, o_ref, fst_o_ref, halo_scr = refs
    else:
        x_ref, g_ref, wu_ref, wd_ref, cw_ref, cb_ref, fst_ref, o_ref, fst_o_ref, halo_scr = refs

    @pl.when(pl.program_id(1) == 0)
    def _():
        halo_scr[...] = fst_ref[0]

    x = x_ref[...]
    hf = _rms(x, g_ref[...]).astype(BF16)
    acc = None
    for c in range(D_FF // FFN_CHUNK):
        cs = slice(c * FFN_CHUNK, (c + 1) * FFN_CHUNK)
        u = _dot(hf, wu_ref[:, cs])
        gpre = _dot(hf, wu_ref[:, D_FF + c * FFN_CHUNK:D_FF + (c + 1) * FFN_CHUNK])
        conv, new_halo = _causal_conv(halo_scr[:, cs], gpre, cw_ref[:, cs], F_CONV)
        halo_scr[:, cs] = new_halo
        act = (u * _silu(conv + cb_ref[:, cs])).astype(BF16)
        contrib = _dot(act, wd_ref[cs, :])
        acc = contrib if acc is None else acc + contrib
    fst_o_ref[0, 0] = halo_scr[...]
    y = x + acc
    o_ref[...] = _rms(y, gf_ref[...]) if final_norm else y


def _ffn(x2d, bsz, L, g, wup, wdn, cw, cb, fst, gfinal):
    tr = min(FFN_ROWS_PER_STEP, L)
    nt = L // tr
    assert L % tr == 0 and D_FF % FFN_CHUNK == 0
    final_norm = gfinal is not None
    row = pl.BlockSpec((tr, D_MODEL), lambda b, t: (b * nt + t, 0))
    const = lambda shape: pl.BlockSpec(shape, lambda b, t: (0, 0), pipeline_mode=pl.Buffered(1))
    in_specs = [
        row, const((1, D_MODEL)), const((D_MODEL, 2 * D_FF)), const((D_FF, D_MODEL)),
        const((F_CONV, D_FF)), const((1, D_FF)),
        pl.BlockSpec((1, SUBLANES, D_FF), lambda b, t: (b, 0, 0)),
    ]
    args = [x2d, g, wup, wdn, cw, cb, fst]
    if final_norm:
        in_specs.append(const((1, D_MODEL)))
        args.append(gfinal)
    return pl.pallas_call(
        functools.partial(_ffn_kernel, final_norm=final_norm),
        grid=(bsz, nt),
        in_specs=in_specs,
        out_specs=[row, pl.BlockSpec((1, 1, SUBLANES, D_FF), lambda b, t: (b, t, 0, 0))],
        out_shape=[SDS((bsz * L, D_MODEL), F32), SDS((bsz, nt, SUBLANES, D_FF), F32)],
        scratch_shapes=[pltpu.VMEM((SUBLANES, D_FF), F32)],
        compiler_params=_cparams(("parallel", "arbitrary")),
        name="ffn",
    )(*args)


def _pad_state(st):
    return jnp.pad(st.astype(F32), ((0, 0), (SUBLANES - st.shape[1], 0), (0, 0)))


def _head_row(v, width):
    return jnp.repeat(v.astype(F32), width)[None, :]


def _prep_layer(l, p):
    w_in = p['w_in'][l]
    w_main = jnp.concatenate([w_in[:, a:b] for a, b in _SEGS_MAIN], axis=1).astype(BF16)
    n_small = sum(b - a for a, b in _SEGS_SMALL)
    w_small = jnp.concatenate([w_in[:, a:b] for a, b in _SEGS_SMALL] + [jnp.zeros((D_MODEL, LANES - n_small), F32)],
                              axis=1).astype(BF16)
    prm = jnp.zeros((SUBLANES, LANES), F32)
    prm = prm.at[0, SM_DEC:SM_DEC + B_HEADS].set(p['b_dt_bias'][l])
    prm = prm.at[0, SM_DT:SM_DT + C_HEADS].set(p['c_dt_bias'][l])
    prm = prm.at[1, SM_DEC:SM_DEC + B_HEADS].set(p['b_a_log'][l])
    prm = prm.at[1, SM_DT:SM_DT + C_HEADS].set(p['c_a_log'][l])
    rows = jnp.zeros((SUBLANES, C_INNER), F32)
    rows = rows.at[0].set(_head_row(p['c_dt_bias'][l], C_HEAD_DIM)[0])
    rows = rows.at[1].set(_head_row(p['c_a_log'][l], C_HEAD_DIM)[0])
    rows = rows.at[2].set(_head_row(p['c_d'][l], C_HEAD_DIM)[0])
    rows = rows.at[3].set(p['c_norm'][l].astype(F32))
    return dict(
        norm_mix=p['norm_mix'][l][None, :], w_main=w_main, w_small=w_small,
        rel=p['a_rel_bias'][l],
        b_conv_w=p['b_conv_w'][l], prm=prm, b_norm=p['b_norm'][l][None, :],
        c_conv_w=p['c_conv_w'][l], c_conv_b=p['c_conv_b'][l][None, :], rows=rows,
        wa=p['w_br_a'][l].astype(BF16), wb=p['w_br_b'][l].astype(BF16), wc=p['w_br_c'][l].astype(BF16),
        wo=p['w_out'][l].astype(BF16),
        norm_x=p['norm_x'][l][None, :], norm_mem=p['norm_mem'][l][None, :],
        wxq=p['wx_q'][l].astype(BF16), wxk=p['wx_k'][l].astype(BF16), wxv=p['wx_v'][l].astype(BF16),
        wxo=p['wx_o'][l].astype(BF16),
        norm_ffn=p['norm_ffn'][l][None, :], w_up=p['w_up'][l].astype(BF16), w_down=p['w_down'][l].astype(BF16),
        f_conv_w=p['f_conv_w'][l], f_conv_b=p['f_conv_b'][l][None, :],
    )


def _expand_matrix():
    e = np.zeros((LANES, C_INNER), np.float32)
    for h in range(C_HEADS):
        e[SM_DT + h, h * C_HEAD_DIM:(h + 1) * C_HEAD_DIM] = 1.0
    return jnp.asarray(e)


def _layer(x2d, bsz, L, lw, a_past, b_conv, b_rec, c_conv, c_ssm, f_conv, mk, mv, gfinal, expand):
    z, zs = _in_proj(x2d, lw['norm_mix'], lw['w_main'], lw['w_small'])

    C = min(CHUNK, L)
    G = 2 if L // C >= 2 else 1
    bias = _attn_bias(lw['rel'], C, G)
    past = None
    if a_past is not None:
        past = tuple(t.reshape(bsz, A_WINDOW, A_WIDTH) for t in a_past)
    ya = _band_attn(z, bsz, L, C, G, bias, past)
    keep = min(A_WINDOW, L)
    z3 = z.reshape(bsz, L, NZ)
    a_k = z3[:, L - keep:, A_WIDTH:2 * A_WIDTH].astype(F32).reshape(bsz, keep, A_HEADS, A_HEAD_DIM)
    a_v = z3[:, L - keep:, 2 * A_WIDTH:3 * A_WIDTH].astype(F32).reshape(bsz, keep, A_HEADS, A_HEAD_DIM)

    yb, b_conv_new, b_rec_new = _gdn(z, zs, bsz, L, _pad_state(b_conv), b_rec.astype(F32),
                                     lw['b_conv_w'], lw['prm'], lw['b_norm'])

    sst = jnp.transpose(c_ssm.astype(F32).reshape(bsz, C_GROUPS, C_HEADS // C_GROUPS, C_HEAD_DIM, C_STATE),
                        (0, 1, 4, 2, 3)).reshape(bsz, C_GROUPS, C_STATE, C_GROUP_W)
    yc, c_conv_new, sst_new = _ssd(z, zs, bsz, L, _pad_state(c_conv), sst, lw['c_conv_w'], lw['c_conv_b'],
                                   expand, lw['prm'], lw['rows'])
    c_ssm_new = jnp.transpose(sst_new.reshape(bsz, C_GROUPS, C_STATE, C_HEADS // C_GROUPS, C_HEAD_DIM),
                              (0, 1, 3, 4, 2)).reshape(bsz, C_HEADS, C_HEAD_DIM, C_STATE)

    x2d = _merge(x2d, ya, yb, yc, z, lw['wa'], lw['wb'], lw['wc'], lw['wo'])
    x2d = _xattn(x2d, bsz, L, mk, mv, lw['norm_x'], lw['wxq'], lw['wxo'])

    x2d, fst_new = _ffn(x2d, bsz, L, lw['norm_ffn'], lw['w_up'], lw['w_down'], lw['f_conv_w'], lw['f_conv_b'],
                        _pad_state(f_conv), gfinal)
    f_conv_new = fst_new[:, -1, SUBLANES - (F_CONV - 1):, :]
    return x2d, (a_k, a_v, b_conv_new, b_rec_new, c_conv_new, c_ssm_new, f_conv_new)


def kernel(x_prompt, x_sample, cache_attn_k, cache_attn_v, state_b_conv, state_b_rec, state_c_conv, state_c_ssm, state_ffn_conv, cache_mem_k, cache_mem_v, mem_prompt, norm_mix, w_in, a_rel_bias, b_conv_w, b_a_log, b_dt_bias, b_norm, c_conv_w, c_conv_b, c_dt_bias, c_a_log, c_d, c_norm, w_br_a, w_br_b, w_br_c, w_out, norm_x, norm_mem, wx_q, wx_k, wx_v, wx_o, norm_ffn, w_up, f_conv_w, f_conv_b, w_down, norm_final):
    params = dict(norm_mix=norm_mix, w_in=w_in, a_rel_bias=a_rel_bias, b_conv_w=b_conv_w, b_a_log=b_a_log,
                  b_dt_bias=b_dt_bias, b_norm=b_norm, c_conv_w=c_conv_w, c_conv_b=c_conv_b, c_dt_bias=c_dt_bias,
                  c_a_log=c_a_log, c_d=c_d, c_norm=c_norm, w_br_a=w_br_a, w_br_b=w_br_b, w_br_c=w_br_c,
                  w_out=w_out, norm_x=norm_x, norm_mem=norm_mem, wx_q=wx_q, wx_k=wx_k, wx_v=wx_v, wx_o=wx_o,
                  norm_ffn=norm_ffn, w_up=w_up, f_conv_w=f_conv_w, f_conv_b=f_conv_b, w_down=w_down)
    nb, seq, _ = x_prompt.shape
    db, dseq, _ = x_sample.shape
    expand = _expand_matrix()
    gfin = norm_final[None, :]
    xp = x_prompt.reshape(nb * seq, D_MODEL)
    xs = x_sample.reshape(db * dseq, D_MODEL)
    p_states, s_states, p_mk, p_mv = [], [], [], []
    for l in range(DEPTH):
        lw = _prep_layer(l, params)
        last = gfin if l == DEPTH - 1 else None
        mk, mv = _memory_kv(mem_prompt, lw['norm_mem'], lw['wxk'], lw['wxv'])
        xp, st_p = _layer(
            xp, nb, seq, lw, None,
            jnp.zeros((nb, B_CONV - 1, 3 * B_WIDTH), F32),
            jnp.zeros((nb, B_HEADS, B_HEAD_DIM, B_HEAD_DIM), F32),
            jnp.zeros((nb, C_CONV - 1, C_XBC), F32),
            jnp.zeros((nb, C_HEADS, C_HEAD_DIM, C_STATE), F32),
            jnp.zeros((nb, F_CONV - 1, D_FF), F32),
            mk, mv, last, expand)
        p_states.append(st_p)
        p_mk.append(mk.reshape(nb, N_MEM, X_HEADS, X_HEAD_DIM))
        p_mv.append(mv.reshape(nb, N_MEM, X_HEADS, X_HEAD_DIM))
        xs, st_s = _layer(
            xs, db, dseq, lw, (cache_attn_k[l], cache_attn_v[l]),
            state_b_conv[l], state_b_rec[l], state_c_conv[l], state_c_ssm[l], state_ffn_conv[l],
            cache_mem_k[l].reshape(db, N_MEM, D_MODEL), cache_mem_v[l].reshape(db, N_MEM, D_MODEL),
            last, expand)
        s_states.append(st_s)

    y_prompt = xp.reshape(nb, seq, D_MODEL)
    y_sample = xs.reshape(db, dseq, D_MODEL)
    pst = [jnp.stack([s[i] for s in p_states]) for i in range(7)]
    sst = [jnp.stack([s[i] for s in s_states]) for i in range(7)]
    return (y_prompt, y_sample, pst[0], pst[1], pst[2], pst[3], pst[4], pst[5], pst[6],
            jnp.stack(p_mk), jnp.stack(p_mv),
            sst[0], sst[1], sst[2], sst[3], sst[4], sst[5], sst[6])
```

```python
import functools

import jax
import jax.numpy as jnp
import numpy as np
from jax import lax
from jax.experimental import pallas as pl
from jax.experimental.pallas import tpu as pltpu

F32 = jnp.float32
BF16 = jnp.bfloat16
SDS = jax.ShapeDtypeStruct

D_MODEL = 1024
DEPTH = 2
CHUNK = 64
N_MEM = 256
EPS = 1e-6

A_HEADS = 8
A_HEAD_DIM = 64
A_WIDTH = A_HEADS * A_HEAD_DIM
A_WINDOW = 8 * CHUNK
A_MAX_REL = 128

B_HEADS = 4
B_HEAD_DIM = 128
B_WIDTH = B_HEADS * B_HEAD_DIM
B_CONV = 4

C_HEADS = 16
C_HEAD_DIM = 64
C_INNER = C_HEADS * C_HEAD_DIM
C_GROUPS = 2
C_STATE = 128
C_XBC = C_INNER + 2 * C_GROUPS * C_STATE
C_CONV = 4
C_GROUP_W = C_INNER // C_GROUPS

X_HEADS = 4
X_HEAD_DIM = D_MODEL // X_HEADS

D_FF = 2816
F_CONV = 3

LANES = 128
SUBLANES = 8
NEG = -1e30
INPROJ_ROWS_PER_STEP = 1024
INPROJ_COLS_PER_STEP = 3072
INPROJ_CHUNK = 512
FFN_ROWS_PER_STEP = 1024
FFN_CHUNK = 256
GDN_SEQS_PER_STEP = 4
GDN_ROWS_PER_STEP = 512
ATTN_HEADS_PER_STEP = 4
VMEM_LIMIT = 56 * 1024 * 1024

NZ = 9216
ZW_A = A_WIDTH
ZB_BQKV = 1
ZB_CZ = 3
ZB_CX = 4
ZB_CBC = 10
ZB_BGATE = 11
ZB_GATES = 6
SM_BETA = 0
SM_DEC = 4
SM_DT = 8

_O_BQKV = 3 * A_WIDTH
_O_BETA = _O_BQKV + 3 * B_WIDTH
_O_DEC = _O_BETA + B_HEADS
_O_BGATE = _O_DEC + B_HEADS
_O_CZ = _O_BGATE + B_WIDTH
_O_CXBC = _O_CZ + C_INNER
_O_CDT = _O_CXBC + C_XBC
_O_GATES = _O_CDT + C_HEADS
_SEGS_MAIN = ((0, _O_BETA), (_O_CZ, _O_CXBC), (_O_CXBC, _O_CDT), (_O_BGATE, _O_CZ), (_O_GATES, _O_GATES + 3 * D_MODEL))
_SEGS_SMALL = ((_O_BETA, _O_BGATE), (_O_CDT, _O_GATES))


def _cparams(sem):
    return pltpu.CompilerParams(dimension_semantics=sem, vmem_limit_bytes=VMEM_LIMIT)


def _rms(x, g):
    return x * lax.rsqrt(jnp.mean(x * x, axis=-1, keepdims=True) + EPS) * g


def _silu(x):
    return x * jax.nn.sigmoid(x)


def _softplus(x):
    return jnp.maximum(x, 0.0) + jnp.log1p(jnp.exp(-jnp.abs(x)))


def _dot(a, b):
    return jnp.dot(a, b, preferred_element_type=F32)


def _pieces(a, n):
    out = []
    for _ in range(n - 1):
        p = a.astype(BF16)
        out.append(p)
        a = a - p.astype(F32)
    out.append(a.astype(BF16))
    return out


def _dot_sel(sel, b):
    s16 = sel.astype(BF16)
    b1, b2, b3 = _pieces(b, 3)
    return _dot(s16, b1) + (_dot(s16, b2) + _dot(s16, b3))


def _dot_pick(a, sel):
    s16 = sel.astype(BF16)
    a1, a2, a3 = _pieces(a, 3)
    return _dot(a1, s16) + (_dot(a2, s16) + _dot(a3, s16))


def _dot_x3(a, b):
    ah, al = _pieces(a, 2)
    bh, bl = _pieces(b, 2)
    return _dot(ah, bh) + (_dot(ah, bl) + _dot(al, bh))


def _dot_nt(a, b):
    return lax.dot_general(a, b, (((1,), (1,)), ((), ())), preferred_element_type=F32)


def _dot_tn(a, b):
    return lax.dot_general(a, b, (((0,), (0,)), ((), ())), preferred_element_type=F32)


def _row_start(i, n):
    return i * n if isinstance(i, int) else pl.multiple_of(i * n, n)


def _causal_conv(prev8, cur, w_ref, taps):
    ext = jnp.concatenate([prev8, cur], axis=0)
    acc = cur * w_ref[taps - 1:taps, :]
    for s in range(1, taps):
        acc = acc + pltpu.roll(ext, s, 0)[SUBLANES:, :] * w_ref[taps - 1 - s:taps - s, :]
    return acc, ext[cur.shape[0]:, :]


def _inproj_kernel(x_ref, g_ref, w_ref, ws_ref, z_ref, zs_ref, h_scr):
    @pl.when(pl.program_id(1) == 0)
    def _():
        hb = _rms(x_ref[...], g_ref[...]).astype(BF16)
        h_scr[...] = hb
        zs_ref[...] = _dot(hb, ws_ref[...])

    h = h_scr[...]
    for c in range(w_ref.shape[1] // INPROJ_CHUNK):
        cs = slice(c * INPROJ_CHUNK, (c + 1) * INPROJ_CHUNK)
        z_ref[:, cs] = _dot(h, w_ref[:, cs]).astype(BF16)


def _in_proj(x2d, g, w_main, w_small):
    T = x2d.shape[0]
    tm = min(INPROJ_ROWS_PER_STEP, T)
    tn = INPROJ_COLS_PER_STEP
    assert T % tm == 0 and NZ % tn == 0 and tn % INPROJ_CHUNK == 0
    return pl.pallas_call(
        _inproj_kernel,
        grid=(T // tm, NZ // tn),
        in_specs=[
            pl.BlockSpec((tm, D_MODEL), lambda i, j: (i, 0)),
            pl.BlockSpec((1, D_MODEL), lambda i, j: (0, 0)),
            pl.BlockSpec((D_MODEL, tn), lambda i, j: (0, j)),
            pl.BlockSpec((D_MODEL, LANES), lambda i, j: (0, 0)),
        ],
        out_specs=[
            pl.BlockSpec((tm, tn), lambda i, j: (i, j)),
            pl.BlockSpec((tm, LANES), lambda i, j: (i, 0)),
        ],
        out_shape=[SDS((T, NZ), BF16), SDS((T, LANES), F32)],
        scratch_shapes=[pltpu.VMEM((tm, D_MODEL), BF16)],
        compiler_params=_cparams(("parallel", "arbitrary")),
        name="in_proj",
    )(x2d, g, w_main, w_small)


def _attn_kernel(*refs, L, C, G, has_past):
    if has_past:
        q_ref, k_ref, v_ref, pk_ref, pv_ref, bias_ref, o_ref, kx, vx = refs
        kx[0:A_WINDOW, :] = pk_ref[0].astype(BF16)
        vx[0:A_WINDOW, :] = pv_ref[0].astype(BF16)
    else:
        q_ref, k_ref, v_ref, bias_ref, o_ref, kx, vx = refs
        kx[0:A_WINDOW, :] = jnp.zeros((A_WINDOW, A_WIDTH), BF16)
        vx[0:A_WINDOW, :] = jnp.zeros((A_WINDOW, A_WIDTH), BF16)
    kx[A_WINDOW:A_WINDOW + L, :] = k_ref[...]
    vx[A_WINDOW:A_WINDOW + L, :] = v_ref[...]

    GC = G * C
    NB = A_WINDOW + GC
    lane = lax.broadcasted_iota(jnp.int32, (1, LANES), 1)
    first_head = lane < A_HEAD_DIM
    col = lax.broadcasted_iota(jnp.int32, (1, NB), 1)
    scale = A_HEAD_DIM ** -0.5

    def group(g, carry, masked):
        r0 = _row_start(g, GC)
        q = q_ref[pl.ds(r0, GC), :] * jnp.asarray(scale, BF16)
        kb = kx[pl.ds(r0, NB), :]
        vb = vx[pl.ds(r0, NB), :]
        for h0 in range(0, A_HEADS, ATTN_HEADS_PER_STEP):
            hs = range(h0, h0 + ATTN_HEADS_PER_STEP)
            sl = {h: slice((h // 2) * LANES, (h // 2 + 1) * LANES) for h in hs}
            msk = {h: first_head if h % 2 == 0 else jnp.logical_not(first_head) for h in hs}
            s = {h: _dot_nt(jnp.where(msk[h], q[:, sl[h]], jnp.zeros((GC, LANES), BF16)), kb[:, sl[h]]) for h in hs}
            s = {h: s[h] + bias_ref[h] for h in hs}
            if masked:
                s = {h: jnp.where(r0 + col >= A_WINDOW, s[h], NEG) for h in hs}
            p = {h: jnp.exp(s[h] - jnp.max(s[h], axis=-1, keepdims=True)) for h in hs}
            l = {h: jnp.sum(p[h], axis=-1, keepdims=True) for h in hs}
            pv = {h: _dot(p[h].astype(BF16), vb[:, sl[h]]) / l[h] for h in hs}
            for h in hs:
                if h % 2 == 1:
                    o_ref[pl.ds(r0, GC), sl[h]] = jnp.where(first_head, pv[h - 1], pv[h]).astype(BF16)
        return carry

    ng = L // GC
    n_masked = 0 if has_past else min(ng, -(-A_WINDOW // GC))
    if ng == 1:
        group(0, 0, n_masked > 0)
    else:
        if n_masked:
            lax.fori_loop(0, n_masked, functools.partial(group, masked=True), 0)
        if ng > n_masked:
            lax.fori_loop(n_masked, ng, functools.partial(group, masked=False), 0)


def _attn_bias(table, C, G):
    GC = G * C
    NB = A_WINDOW + GC
    r = np.arange(GC)[:, None]
    j = np.arange(NB)[None, :]
    lo = (r // C) * C
    allowed = (j >= lo) & (j < lo + A_WINDOW + C)
    dmax = A_WINDOW + GC - 1
    rel = np.clip(dmax - np.arange(NB + GC - 1), -A_MAX_REL, A_MAX_REL) + A_MAX_REL
    vflip = table[rel].astype(F32).T
    M = NB + GC - 1
    flat = jnp.tile(jnp.pad(vflip, ((0, 0), (0, 1))), (1, GC))[:, :GC * M]
    b = flat.reshape(A_HEADS, GC, M)[:, :, GC - 1:GC - 1 + NB]
    return jnp.where(allowed[None], b, NEG)


def _band_attn(z, bsz, L, C, G, bias, past):
    T = bsz * L
    has_past = past is not None
    NB = A_WINDOW + G * C
    in_specs = [pl.BlockSpec((L, A_WIDTH), lambda b, i=i: (b, i)) for i in range(3)]
    args = [z, z, z]
    if has_past:
        in_specs += [pl.BlockSpec((1, A_WINDOW, A_WIDTH), lambda b: (b, 0, 0))] * 2
        args += list(past)
    in_specs.append(pl.BlockSpec((A_HEADS, G * C, NB), lambda b: (0, 0, 0)))
    args.append(bias)
    return pl.pallas_call(
        functools.partial(_attn_kernel, L=L, C=C, G=G, has_past=has_past),
        grid=(bsz,),
        in_specs=in_specs,
        out_specs=pl.BlockSpec((L, A_WIDTH), lambda b: (b, 0)),
        out_shape=SDS((T, A_WIDTH), BF16),
        scratch_shapes=[pltpu.VMEM((A_WINDOW + L, A_WIDTH), BF16)] * 2,
        compiler_params=_cparams(("parallel",)),
        name="band_attn",
    )(*args)


def _gdn_kernel(qkv_ref, sm_ref, gate_ref, cst_ref, rst_ref, cw_ref, prm_ref, bn_ref,
                yb_ref, cst_o_ref, rst_o_ref, s_scr, prev_scr, *, nb, lb, cl):
    t = pl.program_id(1)
    nc = lb // cl

    @pl.when(t == 0)
    def _():
        for n in range(nb):
            for h in range(B_HEADS):
                s_scr[n * B_HEADS + h] = rst_ref[n, h]
        prev_scr[...] = cst_ref[...]

    bias_row = prm_ref[0:1, :]
    aneg_row = -jnp.exp(prm_ref[1:2, :])
    bn = bn_ref[...]
    tril = (lax.broadcasted_iota(jnp.int32, (cl, cl), 0) >= lax.broadcasted_iota(jnp.int32, (cl, cl), 1)).astype(F32)
    ri = lax.broadcasted_iota(jnp.int32, (cl, 2 * cl), 0)
    lane2 = lax.broadcasted_iota(jnp.int32, (cl, 2 * cl), 1)
    ci = lane2 & (cl - 1)
    first = lane2 < cl
    incl = ri >= ci
    strict = ri > ci
    eye = (ri == ci).astype(F32)
    sh = int(np.log2(SUBLANES))
    diag_blk = (ri >> sh) == (ci >> sh)
    sub_blk = []
    while (1 << sh) < cl:
        sub_blk.append(((ri >> (sh + 1)) == (ci >> (sh + 1))) & (((ri >> sh) & 1) == 1) & (((ci >> sh) & 1) == 0))
        sh += 1
    U = [(n, h) for n in range(nb) for h in range(B_HEADS)]
    PR = [(U[i], U[i + 1]) for i in range(0, len(U), 2)]
    zk = jnp.zeros((cl, B_HEAD_DIM), BF16)
    zr = jnp.zeros((cl, 2 * B_HEAD_DIM), BF16)

    def bdiag(x):
        z = jnp.zeros_like(x)
        return jnp.concatenate([jnp.where(first, x, z), jnp.where(first, z, x)], axis=0)

    def split_bd(x):
        hi, lo = _pieces(x, 2)
        return bdiag(hi), bdiag(lo)

    def x3(a, bd_hi, bd_lo):
        ah, al = _pieces(a, 2)
        both = _dot(jnp.concatenate([ah, al], axis=0), bd_hi)
        return both[:cl] + (both[cl:] + _dot(ah, bd_lo))

    def body(c, carry):
        r0 = _row_start(c, cl)
        rows = pl.ds(r0, cl)
        act, beta_all, gv = [], [], []
        for n in range(nb):
            conv, new_prev = _causal_conv(prev_scr[n], qkv_ref[n, rows, :].astype(F32), cw_ref, B_CONV)
            prev_scr[n] = new_prev
            act.append(_silu(conv))
            smc = sm_ref[n, rows, :]
            beta_all.append(jax.nn.sigmoid(smc))
            gv.append(_softplus(smc + bias_row) * aneg_row)
        gcs = [_dot_sel(tril, gv[n]) for n in range(nb)]
        q = {(n, h): act[n][:, h * B_HEAD_DIM:(h + 1) * B_HEAD_DIM] for n, h in U}
        k = {(n, h): act[n][:, B_WIDTH + h * B_HEAD_DIM:B_WIDTH + (h + 1) * B_HEAD_DIM] for n, h in U}
        v = {(n, h): act[n][:, 2 * B_WIDTH + h * B_HEAD_DIM:2 * B_WIDTH + (h + 1) * B_HEAD_DIM] for n, h in U}
        q = {u: q[u] * lax.rsqrt(jnp.sum(q[u] * q[u], axis=-1, keepdims=True) + EPS) * (B_HEAD_DIM ** -0.5) for u in U}
        k = {u: k[u] * lax.rsqrt(jnp.sum(k[u] * k[u], axis=-1, keepdims=True) + EPS) for u in U}
        beta = {(n, h): beta_all[n][:, SM_BETA + h:SM_BETA + h + 1] for n, h in U}
        g = {(n, h): gv[n][:, SM_DEC + h:SM_DEC + h + 1] for n, h in U}
        gc = {(n, h): gcs[n][:, SM_DEC + h:SM_DEC + h + 1] for n, h in U}
        gl = {(n, h): gcs[n][cl - 1:cl, SM_DEC + h:SM_DEC + h + 1] for n, h in U}
        e = {pr: _dot_sel(tril, jnp.where(strict, jnp.where(first, g[pr[0]], g[pr[1]]), 0.0)) for pr in PR}
        decay = {pr: jnp.where(incl, jnp.exp(e[pr]), 0.0) for pr in PR}
        kb = {u: k[u] * beta[u] for u in U}
        k16 = {u: k[u].astype(BF16) for u in U}
        kpad = {pr: (jnp.concatenate([k16[pr[0]], zk], axis=0), jnp.concatenate([zk, k16[pr[1]]], axis=0)) for pr in PR}
        a = {pr: _dot_nt(kb[pr[0]].astype(BF16), kpad[pr][0]) + _dot_nt(kb[pr[1]].astype(BF16), kpad[pr][1])
             for pr in PR}
        lm = {pr: jnp.where(strict, a[pr] * decay[pr], 0.0) for pr in PR}
        m = {pr: jnp.where(diag_blk, -lm[pr], 0.0) for pr in PR}
        p = {pr: eye + m[pr] for pr in PR}
        for _ in range(2):
            mb = {pr: split_bd(m[pr]) for pr in PR}
            m = {pr: x3(m[pr], *mb[pr]) for pr in PR}
            mb = {pr: split_bd(m[pr]) for pr in PR}
            p = {pr: p[pr] + x3(p[pr], *mb[pr]) for pr in PR}
        for cmask in sub_blk:
            pb = {pr: split_bd(p[pr]) for pr in PR}
            cx = {pr: x3(jnp.where(cmask, lm[pr], 0.0), *pb[pr]) for pr in PR}
            cb = {pr: split_bd(cx[pr]) for pr in PR}
            p = {pr: p[pr] - x3(p[pr], *cb[pr]) for pr in PR}
        egc = {u: jnp.exp(gc[u]) for u in U}
        rhs = {u: _pieces(jnp.concatenate([v[u] * beta[u], kb[u] * egc[u]], axis=1), 2) for u in U}
        rb = {pr: [jnp.concatenate([jnp.concatenate([rhs[pr[0]][i], zr], axis=1),
                                    jnp.concatenate([zr, rhs[pr[1]][i]], axis=1)], axis=0) for i in range(2)]
              for pr in PR}
        solp = {pr: x3(p[pr], *rb[pr]) for pr in PR}
        sol = {}
        for pr in PR:
            sol[pr[0]] = solp[pr][:, :2 * B_HEAD_DIM]
            sol[pr[1]] = solp[pr][:, 2 * B_HEAD_DIM:]
        qk = {pr: ((_dot_nt(q[pr[0]].astype(BF16), kpad[pr][0]) + _dot_nt(q[pr[1]].astype(BF16), kpad[pr][1]))
                   * decay[pr]).astype(BF16) for pr in PR}
        s0 = {(n, h): s_scr[n * B_HEADS + h] for n, h in U}
        s16 = {u: s0[u].astype(BF16) for u in U}
        ws = {u: _dot(jnp.concatenate([sol[u][:, B_HEAD_DIM:].astype(BF16), (q[u] * egc[u]).astype(BF16)], axis=0),
                      s16[u]) for u in U}
        uu = {u: sol[u][:, :B_HEAD_DIM] - ws[u][:cl] for u in U}
        u16 = {u: uu[u].astype(BF16) for u in U}
        kt = {u: (k[u] * jnp.exp(gl[u] - gc[u])).astype(BF16) for u in U}
        snew = {u: s0[u] * jnp.exp(gl[u]) + _dot_tn(kt[u], u16[u]) for u in U}
        o = {}
        for pr in PR:
            ustack = jnp.concatenate([u16[pr[0]], u16[pr[1]]], axis=0)
            zq = jnp.zeros_like(qk[pr])
            intra = _dot(jnp.concatenate([jnp.where(first, qk[pr], zq), jnp.where(first, zq, qk[pr])], axis=0), ustack)
            o[pr[0]] = ws[pr[0]][cl:] + intra[:cl]
            o[pr[1]] = ws[pr[1]][cl:] + intra[cl:]
        for n, h in U:
            hs = slice(h * B_HEAD_DIM, (h + 1) * B_HEAD_DIM)
            s_scr[n * B_HEADS + h] = snew[n, h]
            gate = gate_ref[n, rows, hs].astype(F32)
            yb_ref[n, rows, hs] = (_rms(o[n, h], bn) * _silu(gate)).astype(BF16)
        return carry

    if nc == 1:
        body(0, 0)
    else:
        lax.fori_loop(0, nc, body, 0)

    @pl.when(t == pl.num_programs(1) - 1)
    def _():
        cst_o_ref[...] = prev_scr[:, SUBLANES - (B_CONV - 1):, :]
        for n in range(nb):
            for h in range(B_HEADS):
                rst_o_ref[n, h] = s_scr[n * B_HEADS + h]


def _gdn(z, zs, bsz, L, cst, rst, cw, prm, bn):
    cl = min(CHUNK, L)
    nb = GDN_SEQS_PER_STEP
    lb = min(GDN_ROWS_PER_STEP, L)
    assert bsz % nb == 0 and L % lb == 0 and lb % cl == 0
    z3 = z.reshape(bsz, L, NZ)
    zs3 = zs.reshape(bsz, L, LANES)
    yb, cst_o, rst_o = pl.pallas_call(
        functools.partial(_gdn_kernel, nb=nb, lb=lb, cl=cl),
        grid=(bsz // nb, L // lb),
        in_specs=[
            pl.BlockSpec((nb, lb, 3 * B_WIDTH), lambda b, t: (b, t, ZB_BQKV)),
            pl.BlockSpec((nb, lb, LANES), lambda b, t: (b, t, 0)),
            pl.BlockSpec((nb, lb, B_WIDTH), lambda b, t: (b, t, ZB_BGATE)),
            pl.BlockSpec((nb, SUBLANES, 3 * B_WIDTH), lambda b, t: (b, 0, 0)),
            pl.BlockSpec((nb, B_HEADS, B_HEAD_DIM, B_HEAD_DIM), lambda b, t: (b, 0, 0, 0)),
            pl.BlockSpec((B_CONV, 3 * B_WIDTH), lambda b, t: (0, 0)),
            pl.BlockSpec((SUBLANES, LANES), lambda b, t: (0, 0)),
            pl.BlockSpec((1, B_HEAD_DIM), lambda b, t: (0, 0)),
        ],
        out_specs=[
            pl.BlockSpec((nb, lb, B_WIDTH), lambda b, t: (b, t, 0)),
            pl.BlockSpec((nb, B_CONV - 1, 3 * B_WIDTH), lambda b, t: (b, 0, 0)),
            pl.BlockSpec((nb, B_HEADS, B_HEAD_DIM, B_HEAD_DIM), lambda b, t: (b, 0, 0, 0)),
        ],
        out_shape=[
            SDS((bsz, L, B_WIDTH), BF16),
            SDS((bsz, B_CONV - 1, 3 * B_WIDTH), F32),
            SDS((bsz, B_HEADS, B_HEAD_DIM, B_HEAD_DIM), F32),
        ],
        scratch_shapes=[
            pltpu.VMEM((nb * B_HEADS, B_HEAD_DIM, B_HEAD_DIM), F32),
            pltpu.VMEM((nb, SUBLANES, 3 * B_WIDTH), F32),
        ],
        compiler_params=_cparams(("parallel", "arbitrary")),
        name="gdn",
    )(z3, zs3, z3, cst, rst, cw, prm, bn)
    return yb.reshape(bsz * L, B_WIDTH), cst_o, rst_o


def _ssd_kernel(cx_ref, cbc_ref, cz_ref, sm_ref, cst_ref, sst_ref, cwx_ref, cwbc_ref, cbx_ref, cbbc_ref,
                expand_ref, prm_ref, rows_ref, yc_ref, cst_o_ref, sst_o_ref, h_scr, *, L, cl):
    nc = L // cl
    h_scr[...] = sst_ref[0]
    bias_row = prm_ref[0:1, :]
    aneg_row = -jnp.exp(prm_ref[1:2, :])
    cdx = rows_ref[2:3, :]
    cn = rows_ref[3:4, :]
    ri = lax.broadcasted_iota(jnp.int32, (cl, cl), 0)
    ci = lax.broadcasted_iota(jnp.int32, (cl, cl), 1)
    tril = (ri >= ci).astype(F32)
    rx = lax.broadcasted_iota(jnp.int32, (cl, C_INNER), 0)
    jx = lax.broadcasted_iota(jnp.int32, (cl, C_INNER), 1) & (C_HEAD_DIM - 1)
    inclx = rx >= jx
    lane = lax.broadcasted_iota(jnp.int32, (1, LANES), 1)
    first_head = lane < C_HEAD_DIM

    def pad_rows(a):
        if cl == CHUNK:
            return a
        return jnp.concatenate([a, jnp.zeros((CHUNK - cl, a.shape[1]), a.dtype)], axis=0)

    def pad_lanes(a):
        if cl == C_HEAD_DIM:
            return a
        return jnp.concatenate([a, jnp.zeros((a.shape[0], C_HEAD_DIM - cl), a.dtype)], axis=1)

    def body(c, carry):
        px, pbc = carry
        r0 = _row_start(c, cl)
        convx, npx = _causal_conv(px, cx_ref[pl.ds(r0, cl), :].astype(F32), cwx_ref, C_CONV)
        convbc, npbc = _causal_conv(pbc, cbc_ref[pl.ds(r0, cl), :].astype(F32), cwbc_ref, C_CONV)
        xs = _silu(convx + cbx_ref[...])
        bcs = _silu(convbc + cbbc_ref[...])
        dt_c = _softplus(sm_ref[pl.ds(r0, cl), :] + bias_row)
        ac_c = _dot_sel(tril, dt_c * aneg_row)
        dtx = _dot_pick(dt_c, expand_ref[...])
        acx = _dot_pick(ac_c, expand_ref[...])
        ac_t = ac_c.T
        rowsj = [pad_lanes(ac_t[SM_DT + h:SM_DT + h + 1, :]) for h in range(C_HEADS)]
        ac_row = jnp.concatenate(rowsj, axis=1)
        decayx = jnp.exp(jnp.where(inclx, acx - ac_row, NEG))
        alast = acx[cl - 1:cl, :]
        eac = jnp.exp(acx)
        ealast = jnp.exp(alast)
        xdt = xs * dtx
        xtil = (xdt * jnp.exp(alast - acx)).astype(BF16)
        ys = []
        for g in range(C_GROUPS):
            gs = slice(g * C_GROUP_W, (g + 1) * C_GROUP_W)
            bg = bcs[:, g * C_STATE:(g + 1) * C_STATE].astype(BF16)
            cg = bcs[:, (C_GROUPS + g) * C_STATE:(C_GROUPS + g + 1) * C_STATE].astype(BF16)
            brep = jnp.concatenate([pad_rows(bg)] * (C_GROUP_W // CHUNK), axis=0)
            w = (_dot_nt(cg, brep) * decayx[:, gs]).astype(BF16)
            hg = h_scr[g]
            yoff = _dot(cg, hg.astype(BF16)) * eac[:, gs]
            yd = []
            for pr in range(C_GROUP_W // LANES):
                lo = g * C_GROUP_W + pr * LANES
                xp = pad_rows(xdt[:, lo:lo + LANES])
                bd = jnp.concatenate([jnp.where(first_head, xp, 0.0), jnp.where(first_head, 0.0, xp)], axis=0)
                yd.append(_dot(w[:, pr * LANES:(pr + 1) * LANES], bd.astype(BF16)))
            h_scr[g] = hg * ealast[:, gs] + _dot_tn(bg, xtil[:, gs])
            ys.append(jnp.concatenate(yd, axis=1) + yoff)
        y = jnp.concatenate(ys, axis=1) + cdx * xs
        t = y * _silu(cz_ref[pl.ds(r0, cl), :].astype(F32))
        yc_ref[pl.ds(r0, cl), :] = _rms(t, cn).astype(BF16)
        return npx, npbc

    carry = (cst_ref[0, :, :C_INNER], cst_ref[0, :, C_INNER:])
    carry = body(0, carry) if nc == 1 else lax.fori_loop(0, nc, body, carry)
    cst_o_ref[0, :, :C_INNER] = carry[0][SUBLANES - (C_CONV - 1):, :]
    cst_o_ref[0, :, C_INNER:] = carry[1][SUBLANES - (C_CONV - 1):, :]
    sst_o_ref[0] = h_scr[...]


def _ssd(z, zs, bsz, L, cst, sst, cw, cb, expand, prm, rows):
    T = bsz * L
    cl = min(CHUNK, L)
    nbc = C_XBC - C_INNER
    full = lambda shape: pl.BlockSpec(shape, lambda b: (0,) * len(shape))
    return pl.pallas_call(
        functools.partial(_ssd_kernel, L=L, cl=cl),
        grid=(bsz,),
        in_specs=[
            pl.BlockSpec((L, C_INNER), lambda b: (b, ZB_CX)),
            pl.BlockSpec((L, nbc), lambda b: (b, ZB_CBC)),
            pl.BlockSpec((L, C_INNER), lambda b: (b, ZB_CZ)),
            pl.BlockSpec((L, LANES), lambda b: (b, 0)),
            pl.BlockSpec((1, SUBLANES, C_XBC), lambda b: (b, 0, 0)),
            pl.BlockSpec((1, C_GROUPS, C_STATE, C_GROUP_W), lambda b: (b, 0, 0, 0)),
            full((C_CONV, C_INNER)),
            full((C_CONV, nbc)),
            full((1, C_INNER)),
            full((1, nbc)),
            full((LANES, C_INNER)),
            full((SUBLANES, LANES)),
            full((SUBLANES, C_INNER)),
        ],
        out_specs=[
            pl.BlockSpec((L, C_INNER), lambda b: (b, 0)),
            pl.BlockSpec((1, C_CONV - 1, C_XBC), lambda b: (b, 0, 0)),
            pl.BlockSpec((1, C_GROUPS, C_STATE, C_GROUP_W), lambda b: (b, 0, 0, 0)),
        ],
        out_shape=[
            SDS((T, C_INNER), BF16),
            SDS((bsz, C_CONV - 1, C_XBC), F32),
            SDS((bsz, C_GROUPS, C_STATE, C_GROUP_W), F32),
        ],
        scratch_shapes=[pltpu.VMEM((C_GROUPS, C_STATE, C_GROUP_W), F32)],
        compiler_params=_cparams(("parallel",)),
        name="ssd",
    )(z, z, z, zs, cst, sst, cw[:, :C_INNER], cw[:, C_INNER:], cb[:, :C_INNER], cb[:, C_INNER:], expand, prm, rows)


def _merge_kernel(x_ref, ya_ref, yb_ref, yc_ref, ga_ref, gb_ref, gc_ref, wa_ref, wb_ref, wc_ref, wo_ref, o_ref):
    m = jax.nn.sigmoid(ga_ref[...].astype(F32)) * _dot(ya_ref[...], wa_ref[...])
    m = m + jax.nn.sigmoid(gb_ref[...].astype(F32)) * _dot(yb_ref[...], wb_ref[...])
    m = m + jax.nn.sigmoid(gc_ref[...].astype(F32)) * _dot(yc_ref[...], wc_ref[...])
    o_ref[...] = x_ref[...] + _dot(m.astype(BF16), wo_ref[...])


def _merge(x2d, ya, yb, yc, z, wa, wb, wc, wo):
    T = x2d.shape[0]
    tm = min(512, T)
    row = lambda w, cb=0: pl.BlockSpec((tm, w), lambda i: (i, cb))
    full = lambda shape: pl.BlockSpec(shape, lambda i: (0, 0))
    return pl.pallas_call(
        _merge_kernel,
        grid=(T // tm,),
        in_specs=[
            row(D_MODEL), row(A_WIDTH), row(B_WIDTH), row(C_INNER),
            row(D_MODEL, ZB_GATES), row(D_MODEL, ZB_GATES + 1), row(D_MODEL, ZB_GATES + 2),
            full((A_WIDTH, D_MODEL)), full((B_WIDTH, D_MODEL)), full((C_INNER, D_MODEL)), full((D_MODEL, D_MODEL)),
        ],
        out_specs=row(D_MODEL),
        out_shape=SDS((T, D_MODEL), F32),
        compiler_params=_cparams(("parallel",)),
        name="merge",
    )(x2d, ya, yb, yc, z, z, z, wa, wb, wc, wo)


def _memkv_kernel(m_ref, g_ref, wk_ref, wv_ref, k_ref, v_ref):
    hm = _rms(m_ref[0], g_ref[...]).astype(BF16)
    k_ref[0] = _dot(hm, wk_ref[...])
    v_ref[0] = _dot(hm, wv_ref[...])


def _memory_kv(mem, g, wk, wv):
    bsz = mem.shape[0]
    blk = pl.BlockSpec((1, N_MEM, D_MODEL), lambda b: (b, 0, 0))
    full = lambda shape: pl.BlockSpec(shape, lambda b: (0, 0))
    return pl.pallas_call(
        _memkv_kernel,
        grid=(bsz,),
        in_specs=[blk, full((1, D_MODEL)), full((D_MODEL, D_MODEL)), full((D_MODEL, D_MODEL))],
        out_specs=[blk, blk],
        out_shape=[SDS((bsz, N_MEM, D_MODEL), F32)] * 2,
        compiler_params=_cparams(("parallel",)),
        name="memory_kv",
    )(mem, g, wk, wv)


def _xattn_kernel(x_ref, mk_ref, mv_ref, g_ref, wq_ref, wo_ref, o_ref, mk16, mv16):
    @pl.when(pl.program_id(1) == 0)
    def _():
        mk16[...] = mk_ref[0].astype(BF16)
        mv16[...] = mv_ref[0].astype(BF16)

    x = x_ref[...]
    hq = _rms(x, g_ref[...]).astype(BF16)
    q = (_dot(hq, wq_ref[...]) * (X_HEAD_DIM ** -0.5)).astype(BF16)
    H = range(X_HEADS)
    hs = [slice(h * X_HEAD_DIM, (h + 1) * X_HEAD_DIM) for h in H]
    s = [_dot_nt(q[:, hs[h]], mk16[:, hs[h]]) for h in H]
    p = [jnp.exp(s[h] - jnp.max(s[h], axis=-1, keepdims=True)) for h in H]
    l = [jnp.sum(p[h], axis=-1, keepdims=True) for h in H]
    o = [(_dot(p[h].astype(BF16), mv16[:, hs[h]]) / l[h]).astype(BF16) for h in H]
    acc = x
    for h in H:
        acc = acc + _dot(o[h], wo_ref[hs[h], :])
    o_ref[...] = acc


def _xattn(x2d, bsz, L, mk, mv, g, wq, wo):
    tq = min(512, L)
    nq = L // tq
    row = pl.BlockSpec((tq, D_MODEL), lambda b, t: (b * nq + t, 0))
    mem = pl.BlockSpec((1, N_MEM, D_MODEL), lambda b, t: (b, 0, 0))
    full = lambda shape: pl.BlockSpec(shape, lambda b, t: (0, 0))
    return pl.pallas_call(
        _xattn_kernel,
        grid=(bsz, nq),
        in_specs=[row, mem, mem, full((1, D_MODEL)), full((D_MODEL, D_MODEL)), full((D_MODEL, D_MODEL))],
        out_specs=row,
        out_shape=SDS((bsz * L, D_MODEL), F32),
        scratch_shapes=[pltpu.VMEM((N_MEM, D_MODEL), BF16)] * 2,
        compiler_params=_cparams(("parallel", "arbitrary")),
        name="xattn",
    )(x2d, mk, mv, g, wq, wo)


def _ffn_kernel(*refs, final_norm):
    if final_norm:
        x_ref, g_ref, wu_ref, wd_ref, cw_ref, cb_ref, fst_ref, gf_ref, o_ref, fst_o_ref, halo_scr = refs
    else:
        x_ref, g_ref, wu_ref, wd_ref, cw_ref, cb_ref, fst_ref, o_ref, fst_o_ref, halo_scr = refs

    @pl.when(pl.program_id(1) == 0)
    def _():
        halo_scr[...] = fst_ref[0]

    x = x_ref[...]
    hf = _rms(x, g_ref[...]).astype(BF16)
    acc = None
    for c in range(D_FF // FFN_CHUNK):
        cs = slice(c * FFN_CHUNK, (c + 1) * FFN_CHUNK)
        u = _dot(hf, wu_ref[:, cs])
        gpre = _dot(hf, wu_ref[:, D_FF + c * FFN_CHUNK:D_FF + (c + 1) * FFN_CHUNK])
        conv, new_halo = _causal_conv(halo_scr[:, cs], gpre, cw_ref[:, cs], F_CONV)
        halo_scr[:, cs] = new_halo
        act = (u * _silu(conv + cb_ref[:, cs])).astype(BF16)
        contrib = _dot(act, wd_ref[cs, :])
        acc = contrib if acc is None else acc + contrib
    fst_o_ref[0, 0] = halo_scr[...]
    y = x + acc
    o_ref[...] = _rms(y, gf_ref[...]) if final_norm else y


def _ffn(x2d, bsz, L, g, wup, wdn, cw, cb, fst, gfinal):
    tr = min(FFN_ROWS_PER_STEP, L)
    nt = L // tr
    assert L % tr == 0 and D_FF % FFN_CHUNK == 0
    final_norm = gfinal is not None
    row = pl.BlockSpec((tr, D_MODEL), lambda b, t: (b * nt + t, 0))
    const = lambda shape: pl.BlockSpec(shape, lambda b, t: (0, 0), pipeline_mode=pl.Buffered(1))
    in_specs = [
        row, const((1, D_MODEL)), const((D_MODEL, 2 * D_FF)), const((D_FF, D_MODEL)),
        const((F_CONV, D_FF)), const((1, D_FF)),
        pl.BlockSpec((1, SUBLANES, D_FF), lambda b, t: (b, 0, 0)),
    ]
    args = [x2d, g, wup, wdn, cw, cb, fst]
    if final_norm:
        in_specs.append(const((1, D_MODEL)))
        args.append(gfinal)
    return pl.pallas_call(
        functools.partial(_ffn_kernel, final_norm=final_norm),
        grid=(bsz, nt),
        in_specs=in_specs,
        out_specs=[row, pl.BlockSpec((1, 1, SUBLANES, D_FF), lambda b, t: (b, t, 0, 0))],
        out_shape=[SDS((bsz * L, D_MODEL), F32), SDS((bsz, nt, SUBLANES, D_FF), F32)],
        scratch_shapes=[pltpu.VMEM((SUBLANES, D_FF), F32)],
        compiler_params=_cparams(("parallel", "arbitrary")),
        name="ffn",
    )(*args)


def _pad_state(st):
    return jnp.pad(st.astype(F32), ((0, 0), (SUBLANES - st.shape[1], 0), (0, 0)))


def _head_row(v, width):
    return jnp.repeat(v.astype(F32), width)[None, :]


def _prep_layer(l, p):
    w_in = p['w_in'][l]
    w_main = jnp.concatenate([w_in[:, a:b] for a, b in _SEGS_MAIN], axis=1).astype(BF16)
    n_small = sum(b - a for a, b in _SEGS_SMALL)
    w_small = jnp.concatenate([w_in[:, a:b] for a, b in _SEGS_SMALL] + [jnp.zeros((D_MODEL, LANES - n_small), F32)],
                              axis=1).astype(BF16)
    prm = jnp.zeros((SUBLANES, LANES), F32)
    prm = prm.at[0, SM_DEC:SM_DEC + B_HEADS].set(p['b_dt_bias'][l])
    prm = prm.at[0, SM_DT:SM_DT + C_HEADS].set(p['c_dt_bias'][l])
    prm = prm.at[1, SM_DEC:SM_DEC + B_HEADS].set(p['b_a_log'][l])
    prm = prm.at[1, SM_DT:SM_DT + C_HEADS].set(p['c_a_log'][l])
    rows = jnp.zeros((SUBLANES, C_INNER), F32)
    rows = rows.at[0].set(_head_row(p['c_dt_bias'][l], C_HEAD_DIM)[0])
    rows = rows.at[1].set(_head_row(p['c_a_log'][l], C_HEAD_DIM)[0])
    rows = rows.at[2].set(_head_row(p['c_d'][l], C_HEAD_DIM)[0])
    rows = rows.at[3].set(p['c_norm'][l].astype(F32))
    return dict(
        norm_mix=p['norm_mix'][l][None, :], w_main=w_main, w_small=w_small,
        rel=p['a_rel_bias'][l],
        b_conv_w=p['b_conv_w'][l], prm=prm, b_norm=p['b_norm'][l][None, :],
        c_conv_w=p['c_conv_w'][l], c_conv_b=p['c_conv_b'][l][None, :], rows=rows,
        wa=p['w_br_a'][l].astype(BF16), wb=p['w_br_b'][l].astype(BF16), wc=p['w_br_c'][l].astype(BF16),
        wo=p['w_out'][l].astype(BF16),
        norm_x=p['norm_x'][l][None, :], norm_mem=p['norm_mem'][l][None, :],
        wxq=p['wx_q'][l].astype(BF16), wxk=p['wx_k'][l].astype(BF16), wxv=p['wx_v'][l].astype(BF16),
        wxo=p['wx_o'][l].astype(BF16),
        norm_ffn=p['norm_ffn'][l][None, :], w_up=p['w_up'][l].astype(BF16), w_down=p['w_down'][l].astype(BF16),
        f_conv_w=p['f_conv_w'][l], f_conv_b=p['f_conv_b'][l][None, :],
    )


def _expand_matrix():
    e = np.zeros((LANES, C_INNER), np.float32)
    for h in range(C_HEADS):
        e[SM_DT + h, h * C_HEAD_DIM:(h + 1) * C_HEAD_DIM] = 1.0
    return jnp.asarray(e)


def _layer(x2d, bsz, L, lw, a_past, b_conv, b_rec, c_conv, c_ssm, f_conv, mk, mv, gfinal, expand):
    z, zs = _in_proj(x2d, lw['norm_mix'], lw['w_main'], lw['w_small'])

    C = min(CHUNK, L)
    G = 2 if L // C >= 2 else 1
    bias = _attn_bias(lw['rel'], C, G)
    past = None
    if a_past is not None:
        past = tuple(t.reshape(bsz, A_WINDOW, A_WIDTH) for t in a_past)
    ya = _band_attn(z, bsz, L, C, G, bias, past)
    keep = min(A_WINDOW, L)
    z3 = z.reshape(bsz, L, NZ)
    a_k = z3[:, L - keep:, A_WIDTH:2 * A_WIDTH].astype(F32).reshape(bsz, keep, A_HEADS, A_HEAD_DIM)
    a_v = z3[:, L - keep:, 2 * A_WIDTH:3 * A_WIDTH].astype(F32).reshape(bsz, keep, A_HEADS, A_HEAD_DIM)

    yb, b_conv_new, b_rec_new = _gdn(z, zs, bsz, L, _pad_state(b_conv), b_rec.astype(F32),
                                     lw['b_conv_w'], lw['prm'], lw['b_norm'])

    sst = jnp.transpose(c_ssm.astype(F32).reshape(bsz, C_GROUPS, C_HEADS // C_GROUPS, C_HEAD_DIM, C_STATE),
                        (0, 1, 4, 2, 3)).reshape(bsz, C_GROUPS, C_STATE, C_GROUP_W)
    yc, c_conv_new, sst_new = _ssd(z, zs, bsz, L, _pad_state(c_conv), sst, lw['c_conv_w'], lw['c_conv_b'],
                                   expand, lw['prm'], lw['rows'])
    c_ssm_new = jnp.transpose(sst_new.reshape(bsz, C_GROUPS, C_STATE, C_HEADS // C_GROUPS, C_HEAD_DIM),
                              (0, 1, 3, 4, 2)).reshape(bsz, C_HEADS, C_HEAD_DIM, C_STATE)

    x2d = _merge(x2d, ya, yb, yc, z, lw['wa'], lw['wb'], lw['wc'], lw['wo'])
    x2d = _xattn(x2d, bsz, L, mk, mv, lw['norm_x'], lw['wxq'], lw['wxo'])

    x2d, fst_new = _ffn(x2d, bsz, L, lw['norm_ffn'], lw['w_up'], lw['w_down'], lw['f_conv_w'], lw['f_conv_b'],
                        _pad_state(f_conv), gfinal)
    f_conv_new = fst_new[:, -1, SUBLANES - (F_CONV - 1):, :]
    return x2d, (a_k, a_v, b_conv_new, b_rec_new, c_conv_new, c_ssm_new, f_conv_new)


def kernel(x_prompt, x_sample, cache_attn_k, cache_attn_v, state_b_conv, state_b_rec, state_c_conv, state_c_ssm, state_ffn_conv, cache_mem_k, cache_mem_v, mem_prompt, norm_mix, w_in, a_rel_bias, b_conv_w, b_a_log, b_dt_bias, b_norm, c_conv_w, c_conv_b, c_dt_bias, c_a_log, c_d, c_norm, w_br_a, w_br_b, w_br_c, w_out, norm_x, norm_mem, wx_q, wx_k, wx_v, wx_o, norm_ffn, w_up, f_conv_w, f_conv_b, w_down, norm_final):
    params = dict(norm_mix=norm_mix, w_in=w_in, a_rel_bias=a_rel_bias, b_conv_w=b_conv_w, b_a_log=b_a_log,
                  b_dt_bias=b_dt_bias, b_norm=b_norm, c_conv_w=c_conv_w, c_conv_b=c_conv_b, c_dt_bias=c_dt_bias,
                  c_a_log=c_a_log, c_d=c_d, c_norm=c_norm, w_br_a=w_br_a, w_br_b=w_br_b, w_br_c=w_br_c,
                  w_out=w_out, norm_x=norm_x, norm_mem=norm_mem, wx_q=wx_q, wx_k=wx_k, wx_v=wx_v, wx_o=wx_o,
                  norm_ffn=norm_ffn, w_up=w_up, f_conv_w=f_conv_w, f_conv_b=f_conv_b, w_down=w_down)
    nb, seq, _ = x_prompt.shape
    db, dseq, _ = x_sample.shape
    expand = _expand_matrix()
    gfin = norm_final[None, :]
    xp = x_prompt.reshape(nb * seq, D_MODEL)
    xs = x_sample.reshape(db * dseq, D_MODEL)
    p_states, s_states, p_mk, p_mv = [], [], [], []
    for l in range(DEPTH):
        lw = _prep_layer(l, params)
        last = gfin if l == DEPTH - 1 else None
        mk, mv = _memory_kv(mem_prompt, lw['norm_mem'], lw['wxk'], lw['wxv'])
        xp, st_p = _layer(
            xp, nb, seq, lw, None,
            jnp.zeros((nb, B_CONV - 1, 3 * B_WIDTH), F32),
            jnp.zeros((nb, B_HEADS, B_HEAD_DIM, B_HEAD_DIM), F32),
            jnp.zeros((nb, C_CONV - 1, C_XBC), F32),
            jnp.zeros((nb, C_HEADS, C_HEAD_DIM, C_STATE), F32),
            jnp.zeros((nb, F_CONV - 1, D_FF), F32),
            mk, mv, last, expand)
        p_states.append(st_p)
        p_mk.append(mk.reshape(nb, N_MEM, X_HEADS, X_HEAD_DIM))
        p_mv.append(mv.reshape(nb, N_MEM, X_HEADS, X_HEAD_DIM))
        xs, st_s = _layer(
            xs, db, dseq, lw, (cache_attn_k[l], cache_attn_v[l]),
            state_b_conv[l], state_b_rec[l], state_c_conv[l], state_c_ssm[l], state_ffn_conv[l],
            cache_mem_k[l].reshape(db, N_MEM, D_MODEL), cache_mem_v[l].reshape(db, N_MEM, D_MODEL),
            last, expand)
        s_states.append(st_s)

    y_prompt = xp.reshape(nb, seq, D_MODEL)
    y_sample = xs.reshape(db, dseq, D_MODEL)
    pst = [jnp.stack([s[i] for s in p_states]) for i in range(7)]
    sst = [jnp.stack([s[i] for s in s_states]) for i in range(7)]
    return (y_prompt, y_sample, pst[0], pst[1], pst[2], pst[3], pst[4], pst[5], pst[6],
            jnp.stack(p_mk), jnp.stack(p_mv),
            sst[0], sst[1], sst[2], sst[3], sst[4], sst[5], sst[6])
```

```python
import functools

import jax
import jax.numpy as jnp
import numpy as np
from jax import lax
from jax.experimental import pallas as pl
from jax.experimental.pallas import tpu as pltpu

F32 = jnp.float32
BF16 = jnp.bfloat16
SDS = jax.ShapeDtypeStruct

D_MODEL = 1024
DEPTH = 2
CHUNK = 64
N_MEM = 256
EPS = 1e-6

A_HEADS = 8
A_HEAD_DIM = 64
A_WIDTH = A_HEADS * A_HEAD_DIM
A_WINDOW = 8 * CHUNK
A_MAX_REL = 128

B_HEADS = 4
B_HEAD_DIM = 128
B_WIDTH = B_HEADS * B_HEAD_DIM
B_CONV = 4

C_HEADS = 16
C_HEAD_DIM = 64
C_INNER = C_HEADS * C_HEAD_DIM
C_GROUPS = 2
C_STATE = 128
C_XBC = C_INNER + 2 * C_GROUPS * C_STATE
C_CONV = 4
C_GROUP_W = C_INNER // C_GROUPS

X_HEADS = 4
X_HEAD_DIM = D_MODEL // X_HEADS

D_FF = 2816
F_CONV = 3

LANES = 128
SUBLANES = 8
NEG = -1e30
INPROJ_ROWS_PER_STEP = 1024
INPROJ_COLS_PER_STEP = 3072
INPROJ_CHUNK = 512
XATTN_ROWS_PER_STEP = 1024
FFN_ROWS_PER_STEP = 1024
FFN_CHUNK = 256
GDN_SEQS_PER_STEP = 4
GDN_ROWS_PER_STEP = 512
ATTN_HEADS_PER_STEP = 2
VMEM_LIMIT = 56 * 1024 * 1024

NZ = 9216
ZB_BQKV = 1
ZB_CZ = 3
ZB_CX = 4
ZB_CBC = 10
ZB_BGATE = 11
ZB_GATES = 6
SM_BETA = 0
SM_DEC = 4
SM_DT = 8

_O_BQKV = 3 * A_WIDTH
_O_BETA = _O_BQKV + 3 * B_WIDTH
_O_DEC = _O_BETA + B_HEADS
_O_BGATE = _O_DEC + B_HEADS
_O_CZ = _O_BGATE + B_WIDTH
_O_CXBC = _O_CZ + C_INNER
_O_CDT = _O_CXBC + C_XBC
_O_GATES = _O_CDT + C_HEADS
_SEGS_MAIN = ((0, _O_BETA), (_O_CZ, _O_CXBC), (_O_CXBC, _O_CDT), (_O_BGATE, _O_CZ), (_O_GATES, _O_GATES + 3 * D_MODEL))
_SEGS_SMALL = ((_O_BETA, _O_BGATE), (_O_CDT, _O_GATES))


def _cparams(sem):
    return pltpu.CompilerParams(dimension_semantics=sem, vmem_limit_bytes=VMEM_LIMIT)


def _rms(x, g):
    return x * lax.rsqrt(jnp.mean(x * x, axis=-1, keepdims=True) + EPS) * g


def _silu(x):
    return x * jax.nn.sigmoid(x)


def _softplus(x):
    return jnp.maximum(x, 0.0) + jnp.log1p(jnp.exp(-jnp.abs(x)))


def _dot(a, b):
    return jnp.dot(a, b, preferred_element_type=F32)


def _pieces(a, n):
    out = []
    for _ in range(n - 1):
        p = a.astype(BF16)
        out.append(p)
        a = a - p.astype(F32)
    out.append(a.astype(BF16))
    return out


def _dot_sel(sel, b):
    s16 = sel.astype(BF16)
    b1, b2, b3 = _pieces(b, 3)
    return _dot(s16, b1) + (_dot(s16, b2) + _dot(s16, b3))


def _dot_pick(a, sel):
    s16 = sel.astype(BF16)
    a1, a2, a3 = _pieces(a, 3)
    return _dot(a1, s16) + (_dot(a2, s16) + _dot(a3, s16))


def _dot_nt(a, b):
    return lax.dot_general(a, b, (((1,), (1,)), ((), ())), preferred_element_type=F32)


def _dot_tn(a, b):
    return lax.dot_general(a, b, (((0,), (0,)), ((), ())), preferred_element_type=F32)


def _row_start(i, n):
    return i * n if isinstance(i, int) else pl.multiple_of(i * n, n)


def _causal_conv(prev8, cur, w_ref, taps):
    ext = jnp.concatenate([prev8, cur], axis=0)
    acc = cur * w_ref[taps - 1:taps, :]
    for s in range(1, taps):
        acc = acc + pltpu.roll(ext, s, 0)[SUBLANES:, :] * w_ref[taps - 1 - s:taps - s, :]
    return acc, ext[cur.shape[0]:, :]


def _inproj_kernel(x_ref, g_ref, w_ref, ws_ref, z_ref, zs_ref, h_scr):
    @pl.when(pl.program_id(1) == 0)
    def _():
        hb = _rms(x_ref[...], g_ref[...]).astype(BF16)
        h_scr[...] = hb
        zs_ref[...] = _dot(hb, ws_ref[...])

    h = h_scr[...]
    for c in range(w_ref.shape[1] // INPROJ_CHUNK):
        cs = slice(c * INPROJ_CHUNK, (c + 1) * INPROJ_CHUNK)
        z_ref[:, cs] = _dot(h, w_ref[:, cs]).astype(BF16)


def _in_proj(x2d, g, w_main, w_small):
    T = x2d.shape[0]
    tm = min(INPROJ_ROWS_PER_STEP, T)
    tn = INPROJ_COLS_PER_STEP
    assert T % tm == 0 and NZ % tn == 0 and tn % INPROJ_CHUNK == 0
    return pl.pallas_call(
        _inproj_kernel,
        grid=(T // tm, NZ // tn),
        in_specs=[
            pl.BlockSpec((tm, D_MODEL), lambda i, j: (i, 0)),
            pl.BlockSpec((1, D_MODEL), lambda i, j: (0, 0)),
            pl.BlockSpec((D_MODEL, tn), lambda i, j: (0, j)),
            pl.BlockSpec((D_MODEL, LANES), lambda i, j: (0, 0)),
        ],
        out_specs=[
            pl.BlockSpec((tm, tn), lambda i, j: (i, j)),
            pl.BlockSpec((tm, LANES), lambda i, j: (i, 0)),
        ],
        out_shape=[SDS((T, NZ), BF16), SDS((T, LANES), F32)],
        scratch_shapes=[pltpu.VMEM((tm, D_MODEL), BF16)],
        compiler_params=_cparams(("parallel", "arbitrary")),
        name="in_proj",
    )(x2d, g, w_main, w_small)


def _attn_kernel(*refs, L, C, G, has_past):
    if has_past:
        q_ref, k_ref, v_ref, pk_ref, pv_ref, bias_ref, o_ref, kx, vx = refs
        kx[0:A_WINDOW, :] = pk_ref[0].astype(BF16)
        vx[0:A_WINDOW, :] = pv_ref[0].astype(BF16)
    else:
        q_ref, k_ref, v_ref, bias_ref, o_ref, kx, vx = refs
        kx[0:A_WINDOW, :] = jnp.zeros((A_WINDOW, A_WIDTH), BF16)
        vx[0:A_WINDOW, :] = jnp.zeros((A_WINDOW, A_WIDTH), BF16)
    kx[A_WINDOW:A_WINDOW + L, :] = k_ref[...]
    vx[A_WINDOW:A_WINDOW + L, :] = v_ref[...]

    GC = G * C
    NB = A_WINDOW + GC
    lane = lax.broadcasted_iota(jnp.int32, (1, LANES), 1)
    first_head = lane < A_HEAD_DIM
    col = lax.broadcasted_iota(jnp.int32, (1, NB), 1)
    scale = A_HEAD_DIM ** -0.5

    def group(g, carry, masked):
        r0 = _row_start(g, GC)
        q = q_ref[pl.ds(r0, GC), :] * jnp.asarray(scale, BF16)
        kb = kx[pl.ds(r0, NB), :]
        vb = vx[pl.ds(r0, NB), :]
        sl = [slice((h // 2) * LANES, (h // 2 + 1) * LANES) for h in range(A_HEADS)]
        batches = [range(h0, h0 + ATTN_HEADS_PER_STEP) for h0 in range(0, A_HEADS, ATTN_HEADS_PER_STEP)]

        def scores(hs):
            msk = {h: first_head if h % 2 == 0 else jnp.logical_not(first_head) for h in hs}
            s = {h: _dot_nt(jnp.where(msk[h], q[:, sl[h]], jnp.zeros((GC, LANES), BF16)), kb[:, sl[h]]) for h in hs}
            s = {h: s[h] + bias_ref[h] for h in hs}
            if masked:
                s = {h: jnp.where(r0 + col >= A_WINDOW, s[h], NEG) for h in hs}
            return s

        def softmax(hs, s):
            p = {h: jnp.exp(s[h] - jnp.max(s[h], axis=-1, keepdims=True)) for h in hs}
            return p, {h: jnp.sum(p[h], axis=-1, keepdims=True) for h in hs}

        def values(hs, p, l):
            pv = {h: _dot(p[h].astype(BF16), vb[:, sl[h]]) / l[h] for h in hs}
            for h in hs:
                if h % 2 == 1:
                    o_ref[pl.ds(r0, GC), sl[h]] = jnp.where(first_head, pv[h - 1], pv[h]).astype(BF16)

        s = scores(batches[0])
        for i, hs in enumerate(batches):
            p, l = softmax(hs, s)
            if i + 1 < len(batches):
                s = scores(batches[i + 1])
            values(hs, p, l)
        return carry

    ng = L // GC
    n_masked = 0 if has_past else min(ng, -(-A_WINDOW // GC))
    if ng == 1:
        group(0, 0, n_masked > 0)
    else:
        if n_masked:
            lax.fori_loop(0, n_masked, functools.partial(group, masked=True), 0)
        if ng > n_masked:
            lax.fori_loop(n_masked, ng, functools.partial(group, masked=False), 0)


def _attn_bias(table, C, G):
    GC = G * C
    NB = A_WINDOW + GC
    r = np.arange(GC)[:, None]
    j = np.arange(NB)[None, :]
    lo = (r // C) * C
    allowed = (j >= lo) & (j < lo + A_WINDOW + C)
    dmax = A_WINDOW + GC - 1
    rel = np.clip(dmax - np.arange(NB + GC - 1), -A_MAX_REL, A_MAX_REL) + A_MAX_REL
    vflip = table[rel].astype(F32).T
    M = NB + GC - 1
    flat = jnp.tile(jnp.pad(vflip, ((0, 0), (0, 1))), (1, GC))[:, :GC * M]
    b = flat.reshape(A_HEADS, GC, M)[:, :, GC - 1:GC - 1 + NB]
    return jnp.where(allowed[None], b, NEG)


def _band_attn(z, bsz, L, C, G, bias, past):
    T = bsz * L
    has_past = past is not None
    NB = A_WINDOW + G * C
    in_specs = [pl.BlockSpec((L, A_WIDTH), lambda b, i=i: (b, i)) for i in range(3)]
    args = [z, z, z]
    if has_past:
        in_specs += [pl.BlockSpec((1, A_WINDOW, A_WIDTH), lambda b: (b, 0, 0))] * 2
        args += list(past)
    in_specs.append(pl.BlockSpec((A_HEADS, G * C, NB), lambda b: (0, 0, 0)))
    args.append(bias)
    return pl.pallas_call(
        functools.partial(_attn_kernel, L=L, C=C, G=G, has_past=has_past),
        grid=(bsz,),
        in_specs=in_specs,
        out_specs=pl.BlockSpec((L, A_WIDTH), lambda b: (b, 0)),
        out_shape=SDS((T, A_WIDTH), BF16),
        scratch_shapes=[pltpu.VMEM((A_WINDOW + L, A_WIDTH), BF16)] * 2,
        compiler_params=_cparams(("parallel",)),
        name="band_attn",
    )(*args)


def _gdn_kernel(qkv_ref, sm_ref, gate_ref, cst_ref, rst_ref, cw_ref, prm_ref, bn_ref,
                yb_ref, cst_o_ref, rst_o_ref, s_scr, prev_scr, *, nb, lb, cl):
    t = pl.program_id(1)
    nc = lb // cl

    @pl.when(t == 0)
    def _():
        for n in range(nb):
            for h in range(B_HEADS):
                s_scr[n * B_HEADS + h] = rst_ref[n, h]
        prev_scr[...] = cst_ref[...]

    bias_row = prm_ref[0:1, :]
    aneg_row = -jnp.exp(prm_ref[1:2, :])
    bn = bn_ref[...]
    tril = (lax.broadcasted_iota(jnp.int32, (cl, cl), 0) >= lax.broadcasted_iota(jnp.int32, (cl, cl), 1)).astype(F32)
    ri = lax.broadcasted_iota(jnp.int32, (cl, 2 * cl), 0)
    lane2 = lax.broadcasted_iota(jnp.int32, (cl, 2 * cl), 1)
    ci = lane2 & (cl - 1)
    first = lane2 < cl
    incl = ri >= ci
    strict = ri > ci
    eye = (ri == ci).astype(F32)
    sh = int(np.log2(SUBLANES))
    diag_blk = (ri >> sh) == (ci >> sh)
    sub_blk = []
    while (1 << sh) < cl:
        sub_blk.append(((ri >> (sh + 1)) == (ci >> (sh + 1))) & (((ri >> sh) & 1) == 1) & (((ci >> sh) & 1) == 0))
        sh += 1
    U = [(n, h) for n in range(nb) for h in range(B_HEADS)]
    PR = [(U[i], U[i + 1]) for i in range(0, len(U), 2)]
    zk = jnp.zeros((cl, B_HEAD_DIM), BF16)
    zr = jnp.zeros((cl, 2 * B_HEAD_DIM), BF16)

    def bdiag(x):
        z = jnp.zeros_like(x)
        return jnp.concatenate([jnp.where(first, x, z), jnp.where(first, z, x)], axis=0)

    def split_bd(x):
        hi, lo = _pieces(x, 2)
        return bdiag(hi), bdiag(lo)

    def x3(a, bd_hi, bd_lo):
        ah, al = _pieces(a, 2)
        both = _dot(jnp.concatenate([ah, al], axis=0), bd_hi)
        return both[:cl] + (both[cl:] + _dot(ah, bd_lo))

    def body(c, carry):
        r0 = _row_start(c, cl)
        rows = pl.ds(r0, cl)
        act, beta_all, gv = [], [], []
        for n in range(nb):
            conv, new_prev = _causal_conv(prev_scr[n], qkv_ref[n, rows, :].astype(F32), cw_ref, B_CONV)
            prev_scr[n] = new_prev
            act.append(_silu(conv))
            smc = sm_ref[n, rows, :]
            beta_all.append(jax.nn.sigmoid(smc))
            gv.append(_softplus(smc + bias_row) * aneg_row)
        gcs = [_dot_sel(tril, gv[n]) for n in range(nb)]
        q = {(n, h): act[n][:, h * B_HEAD_DIM:(h + 1) * B_HEAD_DIM] for n, h in U}
        k = {(n, h): act[n][:, B_WIDTH + h * B_HEAD_DIM:B_WIDTH + (h + 1) * B_HEAD_DIM] for n, h in U}
        v = {(n, h): act[n][:, 2 * B_WIDTH + h * B_HEAD_DIM:2 * B_WIDTH + (h + 1) * B_HEAD_DIM] for n, h in U}
        q = {u: q[u] * lax.rsqrt(jnp.sum(q[u] * q[u], axis=-1, keepdims=True) + EPS) * (B_HEAD_DIM ** -0.5) for u in U}
        k = {u: k[u] * lax.rsqrt(jnp.sum(k[u] * k[u], axis=-1, keepdims=True) + EPS) for u in U}
        beta = {(n, h): beta_all[n][:, SM_BETA + h:SM_BETA + h + 1] for n, h in U}
        g = {(n, h): gv[n][:, SM_DEC + h:SM_DEC + h + 1] for n, h in U}
        gc = {(n, h): gcs[n][:, SM_DEC + h:SM_DEC + h + 1] for n, h in U}
        gl = {(n, h): gcs[n][cl - 1:cl, SM_DEC + h:SM_DEC + h + 1] for n, h in U}
        e = {pr: _dot_sel(tril, jnp.where(strict, jnp.where(first, g[pr[0]], g[pr[1]]), 0.0)) for pr in PR}
        decay = {pr: jnp.where(incl, jnp.exp(e[pr]), 0.0) for pr in PR}
        kb = {u: k[u] * beta[u] for u in U}
        k16 = {u: k[u].astype(BF16) for u in U}
        kpad = {pr: (jnp.concatenate([k16[pr[0]], zk], axis=0), jnp.concatenate([zk, k16[pr[1]]], axis=0)) for pr in PR}
        a = {pr: _dot_nt(kb[pr[0]].astype(BF16), kpad[pr][0]) + _dot_nt(kb[pr[1]].astype(BF16), kpad[pr][1])
             for pr in PR}
        lm = {pr: jnp.where(strict, a[pr] * decay[pr], 0.0) for pr in PR}
        m = {pr: jnp.where(diag_blk, -lm[pr], 0.0) for pr in PR}
        p = {pr: eye + m[pr] for pr in PR}
        for _ in range(2):
            mb = {pr: split_bd(m[pr]) for pr in PR}
            m = {pr: x3(m[pr], *mb[pr]) for pr in PR}
            mb = {pr: split_bd(m[pr]) for pr in PR}
            p = {pr: p[pr] + x3(p[pr], *mb[pr]) for pr in PR}
        for cmask in sub_blk:
            pb = {pr: split_bd(p[pr]) for pr in PR}
            cx = {pr: x3(jnp.where(cmask, lm[pr], 0.0), *pb[pr]) for pr in PR}
            cb = {pr: split_bd(cx[pr]) for pr in PR}
            p = {pr: p[pr] - x3(p[pr], *cb[pr]) for pr in PR}
        egc = {u: jnp.exp(gc[u]) for u in U}
        rhs = {u: _pieces(jnp.concatenate([v[u] * beta[u], kb[u] * egc[u]], axis=1), 2) for u in U}
        rb = {pr: [jnp.concatenate([jnp.concatenate([rhs[pr[0]][i], zr], axis=1),
                                    jnp.concatenate([zr, rhs[pr[1]][i]], axis=1)], axis=0) for i in range(2)]
              for pr in PR}
        solp = {pr: x3(p[pr], *rb[pr]) for pr in PR}
        sol = {}
        for pr in PR:
            sol[pr[0]] = solp[pr][:, :2 * B_HEAD_DIM]
            sol[pr[1]] = solp[pr][:, 2 * B_HEAD_DIM:]
        qk = {pr: ((_dot_nt(q[pr[0]].astype(BF16), kpad[pr][0]) + _dot_nt(q[pr[1]].astype(BF16), kpad[pr][1]))
                   * decay[pr]).astype(BF16) for pr in PR}
        s0 = {(n, h): s_scr[n * B_HEADS + h] for n, h in U}
        s16 = {u: s0[u].astype(BF16) for u in U}
        ws = {u: _dot(jnp.concatenate([sol[u][:, B_HEAD_DIM:].astype(BF16), (q[u] * egc[u]).astype(BF16)], axis=0),
                      s16[u]) for u in U}
        uu = {u: sol[u][:, :B_HEAD_DIM] - ws[u][:cl] for u in U}
        u16 = {u: uu[u].astype(BF16) for u in U}
        kt = {u: (k[u] * jnp.exp(gl[u] - gc[u])).astype(BF16) for u in U}
        snew = {u: s0[u] * jnp.exp(gl[u]) + _dot_tn(kt[u], u16[u]) for u in U}
        o = {}
        for pr in PR:
            ustack = jnp.concatenate([u16[pr[0]], u16[pr[1]]], axis=0)
            zq = jnp.zeros_like(qk[pr])
            intra = _dot(jnp.concatenate([jnp.where(first, qk[pr], zq), jnp.where(first, zq, qk[pr])], axis=0), ustack)
            o[pr[0]] = ws[pr[0]][cl:] + intra[:cl]
            o[pr[1]] = ws[pr[1]][cl:] + intra[cl:]
        for n, h in U:
            hs = slice(h * B_HEAD_DIM, (h + 1) * B_HEAD_DIM)
            s_scr[n * B_HEADS + h] = snew[n, h]
            gate = gate_ref[n, rows, hs].astype(F32)
            yb_ref[n, rows, hs] = (_rms(o[n, h], bn) * _silu(gate)).astype(BF16)
        return carry

    if nc == 1:
        body(0, 0)
    else:
        lax.fori_loop(0, nc, body, 0)

    @pl.when(t == pl.num_programs(1) - 1)
    def _():
        cst_o_ref[...] = prev_scr[:, SUBLANES - (B_CONV - 1):, :]
        for n in range(nb):
            for h in range(B_HEADS):
                rst_o_ref[n, h] = s_scr[n * B_HEADS + h]


def _gdn(z, zs, bsz, L, cst, rst, cw, prm, bn):
    cl = min(CHUNK, L)
    nb = GDN_SEQS_PER_STEP
    lb = min(GDN_ROWS_PER_STEP, L)
    assert bsz % nb == 0 and L % lb == 0 and lb % cl == 0
    z3 = z.reshape(bsz, L, NZ)
    zs3 = zs.reshape(bsz, L, LANES)
    yb, cst_o, rst_o = pl.pallas_call(
        functools.partial(_gdn_kernel, nb=nb, lb=lb, cl=cl),
        grid=(bsz // nb, L // lb),
        in_specs=[
            pl.BlockSpec((nb, lb, 3 * B_WIDTH), lambda b, t: (b, t, ZB_BQKV)),
            pl.BlockSpec((nb, lb, LANES), lambda b, t: (b, t, 0)),
            pl.BlockSpec((nb, lb, B_WIDTH), lambda b, t: (b, t, ZB_BGATE)),
            pl.BlockSpec((nb, SUBLANES, 3 * B_WIDTH), lambda b, t: (b, 0, 0)),
            pl.BlockSpec((nb, B_HEADS, B_HEAD_DIM, B_HEAD_DIM), lambda b, t: (b, 0, 0, 0)),
            pl.BlockSpec((B_CONV, 3 * B_WIDTH), lambda b, t: (0, 0)),
            pl.BlockSpec((SUBLANES, LANES), lambda b, t: (0, 0)),
            pl.BlockSpec((1, B_HEAD_DIM), lambda b, t: (0, 0)),
        ],
        out_specs=[
            pl.BlockSpec((nb, lb, B_WIDTH), lambda b, t: (b, t, 0)),
            pl.BlockSpec((nb, B_CONV - 1, 3 * B_WIDTH), lambda b, t: (b, 0, 0)),
            pl.BlockSpec((nb, B_HEADS, B_HEAD_DIM, B_HEAD_DIM), lambda b, t: (b, 0, 0, 0)),
        ],
        out_shape=[
            SDS((bsz, L, B_WIDTH), BF16),
            SDS((bsz, B_CONV - 1, 3 * B_WIDTH), F32),
            SDS((bsz, B_HEADS, B_HEAD_DIM, B_HEAD_DIM), F32),
        ],
        scratch_shapes=[
            pltpu.VMEM((nb * B_HEADS, B_HEAD_DIM, B_HEAD_DIM), F32),
            pltpu.VMEM((nb, SUBLANES, 3 * B_WIDTH), F32),
        ],
        compiler_params=_cparams(("parallel", "arbitrary")),
        name="gdn",
    )(z3, zs3, z3, cst, rst, cw, prm, bn)
    return yb.reshape(bsz * L, B_WIDTH), cst_o, rst_o


def _ssd_kernel(cx_ref, cbc_ref, cz_ref, sm_ref, cst_ref, sst_ref, cwx_ref, cwbc_ref, cbx_ref, cbbc_ref,
                expand_ref, prm_ref, rows_ref, yc_ref, cst_o_ref, sst_o_ref, h_scr, *, L, cl):
    nc = L // cl
    h_scr[...] = sst_ref[0]
    bias_row = prm_ref[0:1, :]
    aneg_row = -jnp.exp(prm_ref[1:2, :])
    cdx = rows_ref[0:1, :]
    cn = rows_ref[1:2, :]
    ri = lax.broadcasted_iota(jnp.int32, (cl, cl), 0)
    ci = lax.broadcasted_iota(jnp.int32, (cl, cl), 1)
    tril = (ri >= ci).astype(F32)
    rx = lax.broadcasted_iota(jnp.int32, (cl, C_INNER), 0)
    jx = lax.broadcasted_iota(jnp.int32, (cl, C_INNER), 1) & (C_HEAD_DIM - 1)
    inclx = rx >= jx
    lane = lax.broadcasted_iota(jnp.int32, (1, LANES), 1)
    first_head = lane < C_HEAD_DIM

    def pad_rows(a):
        if cl == CHUNK:
            return a
        return jnp.concatenate([a, jnp.zeros((CHUNK - cl, a.shape[1]), a.dtype)], axis=0)

    def pad_lanes(a):
        if cl == C_HEAD_DIM:
            return a
        return jnp.concatenate([a, jnp.zeros((a.shape[0], C_HEAD_DIM - cl), a.dtype)], axis=1)

    def body(c, carry):
        px, pbc = carry
        r0 = _row_start(c, cl)
        convx, npx = _causal_conv(px, cx_ref[pl.ds(r0, cl), :].astype(F32), cwx_ref, C_CONV)
        convbc, npbc = _causal_conv(pbc, cbc_ref[pl.ds(r0, cl), :].astype(F32), cwbc_ref, C_CONV)
        xs = _silu(convx + cbx_ref[...])
        bcs = _silu(convbc + cbbc_ref[...])
        dt_c = _softplus(sm_ref[pl.ds(r0, cl), :] + bias_row)
        ac_c = _dot_sel(tril, dt_c * aneg_row)
        dtx = _dot_pick(dt_c, expand_ref[...])
        acx = _dot_pick(ac_c, expand_ref[...])
        ac_t = ac_c.T
        rowsj = [pad_lanes(ac_t[SM_DT + h:SM_DT + h + 1, :]) for h in range(C_HEADS)]
        ac_row = jnp.concatenate(rowsj, axis=1)
        decayx = jnp.exp(jnp.where(inclx, acx - ac_row, NEG))
        alast = acx[cl - 1:cl, :]
        eac = jnp.exp(acx)
        ealast = jnp.exp(alast)
        xdt = xs * dtx
        xtil = (xdt * jnp.exp(alast - acx)).astype(BF16)
        ys = []
        for g in range(C_GROUPS):
            gs = slice(g * C_GROUP_W, (g + 1) * C_GROUP_W)
            bg = bcs[:, g * C_STATE:(g + 1) * C_STATE].astype(BF16)
            cg = bcs[:, (C_GROUPS + g) * C_STATE:(C_GROUPS + g + 1) * C_STATE].astype(BF16)
            brep = jnp.concatenate([pad_rows(bg)] * (C_GROUP_W // CHUNK), axis=0)
            w = (_dot_nt(cg, brep) * decayx[:, gs]).astype(BF16)
            hg = h_scr[g]
            yoff = _dot(cg, hg.astype(BF16)) * eac[:, gs]
            yd = []
            for pr in range(C_GROUP_W // LANES):
                lo = g * C_GROUP_W + pr * LANES
                xp = pad_rows(xdt[:, lo:lo + LANES])
                bd = jnp.concatenate([jnp.where(first_head, xp, 0.0), jnp.where(first_head, 0.0, xp)], axis=0)
                yd.append(_dot(w[:, pr * LANES:(pr + 1) * LANES], bd.astype(BF16)))
            h_scr[g] = hg * ealast[:, gs] + _dot_tn(bg, xtil[:, gs])
            ys.append(jnp.concatenate(yd, axis=1) + yoff)
        y = jnp.concatenate(ys, axis=1) + cdx * xs
        t = y * _silu(cz_ref[pl.ds(r0, cl), :].astype(F32))
        yc_ref[pl.ds(r0, cl), :] = _rms(t, cn).astype(BF16)
        return npx, npbc

    carry = (cst_ref[0, :, :C_INNER], cst_ref[0, :, C_INNER:])
    carry = body(0, carry) if nc == 1 else lax.fori_loop(0, nc, body, carry)
    cst_o_ref[0, :, :C_INNER] = carry[0][SUBLANES - (C_CONV - 1):, :]
    cst_o_ref[0, :, C_INNER:] = carry[1][SUBLANES - (C_CONV - 1):, :]
    sst_o_ref[0] = h_scr[...]


def _ssd(z, zs, bsz, L, cst, sst, cw, cb, expand, prm, rows):
    T = bsz * L
    cl = min(CHUNK, L)
    nbc = C_XBC - C_INNER
    full = lambda shape: pl.BlockSpec(shape, lambda b: (0,) * len(shape))
    return pl.pallas_call(
        functools.partial(_ssd_kernel, L=L, cl=cl),
        grid=(bsz,),
        in_specs=[
            pl.BlockSpec((L, C_INNER), lambda b: (b, ZB_CX)),
            pl.BlockSpec((L, nbc), lambda b: (b, ZB_CBC)),
            pl.BlockSpec((L, C_INNER), lambda b: (b, ZB_CZ)),
            pl.BlockSpec((L, LANES), lambda b: (b, 0)),
            pl.BlockSpec((1, SUBLANES, C_XBC), lambda b: (b, 0, 0)),
            pl.BlockSpec((1, C_GROUPS, C_STATE, C_GROUP_W), lambda b: (b, 0, 0, 0)),
            full((C_CONV, C_INNER)),
            full((C_CONV, nbc)),
            full((1, C_INNER)),
            full((1, nbc)),
            full((LANES, C_INNER)),
            full((SUBLANES, LANES)),
            full((SUBLANES, C_INNER)),
        ],
        out_specs=[
            pl.BlockSpec((L, C_INNER), lambda b: (b, 0)),
            pl.BlockSpec((1, C_CONV - 1, C_XBC), lambda b: (b, 0, 0)),
            pl.BlockSpec((1, C_GROUPS, C_STATE, C_GROUP_W), lambda b: (b, 0, 0, 0)),
        ],
        out_shape=[
            SDS((T, C_INNER), BF16),
            SDS((bsz, C_CONV - 1, C_XBC), F32),
            SDS((bsz, C_GROUPS, C_STATE, C_GROUP_W), F32),
        ],
        scratch_shapes=[pltpu.VMEM((C_GROUPS, C_STATE, C_GROUP_W), F32)],
        compiler_params=_cparams(("parallel",)),
        name="ssd",
    )(z, z, z, zs, cst, sst, cw[:, :C_INNER], cw[:, C_INNER:], cb[:, :C_INNER], cb[:, C_INNER:], expand, prm, rows)


def _merge_kernel(x_ref, ya_ref, yb_ref, yc_ref, ga_ref, gb_ref, gc_ref, wa_ref, wb_ref, wc_ref, wo_ref, o_ref):
    m = jax.nn.sigmoid(ga_ref[...].astype(F32)) * _dot(ya_ref[...], wa_ref[...])
    m = m + jax.nn.sigmoid(gb_ref[...].astype(F32)) * _dot(yb_ref[...], wb_ref[...])
    m = m + jax.nn.sigmoid(gc_ref[...].astype(F32)) * _dot(yc_ref[...], wc_ref[...])
    o_ref[...] = x_ref[...] + _dot(m.astype(BF16), wo_ref[...])


def _merge(x2d, ya, yb, yc, z, wa, wb, wc, wo):
    T = x2d.shape[0]
    tm = min(512, T)
    row = lambda w, cb=0: pl.BlockSpec((tm, w), lambda i: (i, cb))
    full = lambda shape: pl.BlockSpec(shape, lambda i: (0, 0))
    return pl.pallas_call(
        _merge_kernel,
        grid=(T // tm,),
        in_specs=[
            row(D_MODEL), row(A_WIDTH), row(B_WIDTH), row(C_INNER),
            row(D_MODEL, ZB_GATES), row(D_MODEL, ZB_GATES + 1), row(D_MODEL, ZB_GATES + 2),
            full((A_WIDTH, D_MODEL)), full((B_WIDTH, D_MODEL)), full((C_INNER, D_MODEL)), full((D_MODEL, D_MODEL)),
        ],
        out_specs=row(D_MODEL),
        out_shape=SDS((T, D_MODEL), F32),
        compiler_params=_cparams(("parallel",)),
        name="merge",
    )(x2d, ya, yb, yc, z, z, z, wa, wb, wc, wo)


def _memkv_kernel(m_ref, g_ref, wk_ref, wv_ref, k_ref, v_ref):
    hm = _rms(m_ref[0], g_ref[...]).astype(BF16)
    k_ref[0] = _dot(hm, wk_ref[...])
    v_ref[0] = _dot(hm, wv_ref[...])


def _memory_kv(mem, g, wk, wv):
    bsz = mem.shape[0]
    blk = pl.BlockSpec((1, N_MEM, D_MODEL), lambda b: (b, 0, 0))
    full = lambda shape: pl.BlockSpec(shape, lambda b: (0, 0))
    return pl.pallas_call(
        _memkv_kernel,
        grid=(bsz,),
        in_specs=[blk, full((1, D_MODEL)), full((D_MODEL, D_MODEL)), full((D_MODEL, D_MODEL))],
        out_specs=[blk, blk],
        out_shape=[SDS((bsz, N_MEM, D_MODEL), F32)] * 2,
        compiler_params=_cparams(("parallel",)),
        name="memory_kv",
    )(mem, g, wk, wv)


def _xattn_kernel(x_ref, mk_ref, mv_ref, g_ref, wq_ref, wo_ref, o_ref, mk16, mv16):
    @pl.when(pl.program_id(1) == 0)
    def _():
        mk16[...] = mk_ref[0].astype(BF16)
        mv16[...] = mv_ref[0].astype(BF16)

    x = x_ref[...]
    hq = _rms(x, g_ref[...]).astype(BF16)
    q = (_dot(hq, wq_ref[...]) * (X_HEAD_DIM ** -0.5)).astype(BF16)
    H = range(X_HEADS)
    hs = [slice(h * X_HEAD_DIM, (h + 1) * X_HEAD_DIM) for h in H]
    s = [_dot_nt(q[:, hs[h]], mk16[:, hs[h]]) for h in H]
    p = [jnp.exp(s[h] - jnp.max(s[h], axis=-1, keepdims=True)) for h in H]
    l = [jnp.sum(p[h], axis=-1, keepdims=True) for h in H]
    o = [(_dot(p[h].astype(BF16), mv16[:, hs[h]]) / l[h]).astype(BF16) for h in H]
    acc = x
    for h in H:
        acc = acc + _dot(o[h], wo_ref[hs[h], :])
    o_ref[...] = acc


def _xattn(x2d, bsz, L, mk, mv, g, wq, wo):
    tq = min(XATTN_ROWS_PER_STEP, L)
    assert L % tq == 0
    nq = L // tq
    row = pl.BlockSpec((tq, D_MODEL), lambda b, t: (b * nq + t, 0))
    mem = pl.BlockSpec((1, N_MEM, D_MODEL), lambda b, t: (b, 0, 0))
    full = lambda shape: pl.BlockSpec(shape, lambda b, t: (0, 0))
    return pl.pallas_call(
        _xattn_kernel,
        grid=(bsz, nq),
        in_specs=[row, mem, mem, full((1, D_MODEL)), full((D_MODEL, D_MODEL)), full((D_MODEL, D_MODEL))],
        out_specs=row,
        out_shape=SDS((bsz * L, D_MODEL), F32),
        scratch_shapes=[pltpu.VMEM((N_MEM, D_MODEL), BF16)] * 2,
        compiler_params=_cparams(("parallel", "arbitrary")),
        name="xattn",
    )(x2d, mk, mv, g, wq, wo)


def _ffn_kernel(*refs, final_norm):
    if final_norm:
        x_ref, g_ref, wu_ref, wd_ref, cw_ref, cb_ref, fst_ref, gf_ref, o_ref, fst_o_ref, halo_scr = refs
    else:
        x_ref, g_ref, wu_ref, wd_ref, cw_ref, cb_ref, fst_ref, o_ref, fst_o_ref, halo_scr = refs

    @pl.when(pl.program_id(1) == 0)
    def _():
        halo_scr[...] = fst_ref[0]

    x = x_ref[...]
    hf = _rms(x, g_ref[...]).astype(BF16)
    acc = None
    for c in range(D_FF // FFN_CHUNK):
        cs = slice(c * FFN_CHUNK, (c + 1) * FFN_CHUNK)
        u = _dot(hf, wu_ref[:, cs])
        gpre = _dot(hf, wu_ref[:, D_FF + c * FFN_CHUNK:D_FF + (c + 1) * FFN_CHUNK])
        conv, new_halo = _causal_conv(halo_scr[:, cs], gpre, cw_ref[:, cs], F_CONV)
        halo_scr[:, cs] = new_halo
        act = (u * _silu(conv + cb_ref[:, cs])).astype(BF16)
        contrib = _dot(act, wd_ref[cs, :])
        acc = contrib if acc is None else acc + contrib
    fst_o_ref[0, 0] = halo_scr[...]
    y = x + acc
    o_ref[...] = _rms(y, gf_ref[...]) if final_norm else y


def _ffn(x2d, bsz, L, g, wup, wdn, cw, cb, fst, gfinal):
    tr = min(FFN_ROWS_PER_STEP, L)
    nt = L // tr
    assert L % tr == 0 and D_FF % FFN_CHUNK == 0
    final_norm = gfinal is not None
    row = pl.BlockSpec((tr, D_MODEL), lambda b, t: (b * nt + t, 0))
    const = lambda shape: pl.BlockSpec(shape, lambda b, t: (0, 0), pipeline_mode=pl.Buffered(1))
    in_specs = [
        row, const((1, D_MODEL)), const((D_MODEL, 2 * D_FF)), const((D_FF, D_MODEL)),
        const((F_CONV, D_FF)), const((1, D_FF)),
        pl.BlockSpec((1, SUBLANES, D_FF), lambda b, t: (b, 0, 0)),
    ]
    args = [x2d, g, wup, wdn, cw, cb, fst]
    if final_norm:
        in_specs.append(const((1, D_MODEL)))
        args.append(gfinal)
    return pl.pallas_call(
        functools.partial(_ffn_kernel, final_norm=final_norm),
        grid=(bsz, nt),
        in_specs=in_specs,
        out_specs=[row, pl.BlockSpec((1, 1, SUBLANES, D_FF), lambda b, t: (b, t, 0, 0))],
        out_shape=[SDS((bsz * L, D_MODEL), F32), SDS((bsz, nt, SUBLANES, D_FF), F32)],
        scratch_shapes=[pltpu.VMEM((SUBLANES, D_FF), F32)],
        compiler_params=_cparams(("parallel", "arbitrary")),
        name="ffn",
    )(*args)


def _pad_state(st):
    return jnp.pad(st.astype(F32), ((0, 0), (SUBLANES - st.shape[1], 0), (0, 0)))


def _head_row(v, width):
    return jnp.repeat(v.astype(F32), width)[None, :]


def _prep_layer(l, p):
    w_in = p['w_in'][l]
    w_main = jnp.concatenate([w_in[:, a:b].astype(BF16) for a, b in _SEGS_MAIN], axis=1)
    n_small = sum(b - a for a, b in _SEGS_SMALL)
    w_small = jnp.concatenate([w_in[:, a:b] for a, b in _SEGS_SMALL] + [jnp.zeros((D_MODEL, LANES - n_small), F32)],
                              axis=1).astype(BF16)
    prm = jnp.zeros((SUBLANES, LANES), F32)
    prm = prm.at[0, SM_DEC:SM_DEC + B_HEADS].set(p['b_dt_bias'][l])
    prm = prm.at[0, SM_DT:SM_DT + C_HEADS].set(p['c_dt_bias'][l])
    prm = prm.at[1, SM_DEC:SM_DEC + B_HEADS].set(p['b_a_log'][l])
    prm = prm.at[1, SM_DT:SM_DT + C_HEADS].set(p['c_a_log'][l])
    rows = jnp.zeros((SUBLANES, C_INNER), F32)
    rows = rows.at[0].set(_head_row(p['c_d'][l], C_HEAD_DIM)[0])
    rows = rows.at[1].set(p['c_norm'][l].astype(F32))
    return dict(
        norm_mix=p['norm_mix'][l][None, :], w_main=w_main, w_small=w_small,
        rel=p['a_rel_bias'][l],
        b_conv_w=p['b_conv_w'][l], prm=prm, b_norm=p['b_norm'][l][None, :],
        c_conv_w=p['c_conv_w'][l], c_conv_b=p['c_conv_b'][l][None, :], rows=rows,
        wa=p['w_br_a'][l].astype(BF16), wb=p['w_br_b'][l].astype(BF16), wc=p['w_br_c'][l].astype(BF16),
        wo=p['w_out'][l].astype(BF16),
        norm_x=p['norm_x'][l][None, :], norm_mem=p['norm_mem'][l][None, :],
        wxq=p['wx_q'][l].astype(BF16), wxk=p['wx_k'][l].astype(BF16), wxv=p['wx_v'][l].astype(BF16),
        wxo=p['wx_o'][l].astype(BF16),
        norm_ffn=p['norm_ffn'][l][None, :], w_up=p['w_up'][l].astype(BF16), w_down=p['w_down'][l].astype(BF16),
        f_conv_w=p['f_conv_w'][l], f_conv_b=p['f_conv_b'][l][None, :],
    )


def _expand_matrix():
    e = np.zeros((LANES, C_INNER), np.float32)
    for h in range(C_HEADS):
        e[SM_DT + h, h * C_HEAD_DIM:(h + 1) * C_HEAD_DIM] = 1.0
    return jnp.asarray(e)


def _layer(x2d, bsz, L, lw, a_past, b_conv, b_rec, c_conv, c_ssm, f_conv, mk, mv, gfinal, expand):
    z, zs = _in_proj(x2d, lw['norm_mix'], lw['w_main'], lw['w_small'])

    C = min(CHUNK, L)
    G = 2 if L // C >= 2 else 1
    bias = _attn_bias(lw['rel'], C, G)
    past = None
    if a_past is not None:
        past = tuple(t.reshape(bsz, A_WINDOW, A_WIDTH) for t in a_past)
    ya = _band_attn(z, bsz, L, C, G, bias, past)
    keep = min(A_WINDOW, L)
    z3 = z.reshape(bsz, L, NZ)
    a_k = z3[:, L - keep:, A_WIDTH:2 * A_WIDTH].astype(F32).reshape(bsz, keep, A_HEADS, A_HEAD_DIM)
    a_v = z3[:, L - keep:, 2 * A_WIDTH:3 * A_WIDTH].astype(F32).reshape(bsz, keep, A_HEADS, A_HEAD_DIM)

    yb, b_conv_new, b_rec_new = _gdn(z, zs, bsz, L, _pad_state(b_conv), b_rec.astype(F32),
                                     lw['b_conv_w'], lw['prm'], lw['b_norm'])

    sst = jnp.transpose(c_ssm.astype(F32).reshape(bsz, C_GROUPS, C_HEADS // C_GROUPS, C_HEAD_DIM, C_STATE),
                        (0, 1, 4, 2, 3)).reshape(bsz, C_GROUPS, C_STATE, C_GROUP_W)
    yc, c_conv_new, sst_new = _ssd(z, zs, bsz, L, _pad_state(c_conv), sst, lw['c_conv_w'], lw['c_conv_b'],
                                   expand, lw['prm'], lw['rows'])
    c_ssm_new = jnp.transpose(sst_new.reshape(bsz, C_GROUPS, C_STATE, C_HEADS // C_GROUPS, C_HEAD_DIM),
                              (0, 1, 3, 4, 2)).reshape(bsz, C_HEADS, C_HEAD_DIM, C_STATE)

    x2d = _merge(x2d, ya, yb, yc, z, lw['wa'], lw['wb'], lw['wc'], lw['wo'])
    x2d = _xattn(x2d, bsz, L, mk, mv, lw['norm_x'], lw['wxq'], lw['wxo'])

    x2d, fst_new = _ffn(x2d, bsz, L, lw['norm_ffn'], lw['w_up'], lw['w_down'], lw['f_conv_w'], lw['f_conv_b'],
                        _pad_state(f_conv), gfinal)
    f_conv_new = fst_new[:, -1, SUBLANES - (F_CONV - 1):, :]
    return x2d, (a_k, a_v, b_conv_new, b_rec_new, c_conv_new, c_ssm_new, f_conv_new)


def kernel(x_prompt, x_sample, cache_attn_k, cache_attn_v, state_b_conv, state_b_rec, state_c_conv, state_c_ssm, state_ffn_conv, cache_mem_k, cache_mem_v, mem_prompt, norm_mix, w_in, a_rel_bias, b_conv_w, b_a_log, b_dt_bias, b_norm, c_conv_w, c_conv_b, c_dt_bias, c_a_log, c_d, c_norm, w_br_a, w_br_b, w_br_c, w_out, norm_x, norm_mem, wx_q, wx_k, wx_v, wx_o, norm_ffn, w_up, f_conv_w, f_conv_b, w_down, norm_final):
    params = dict(norm_mix=norm_mix, w_in=w_in, a_rel_bias=a_rel_bias, b_conv_w=b_conv_w, b_a_log=b_a_log,
                  b_dt_bias=b_dt_bias, b_norm=b_norm, c_conv_w=c_conv_w, c_conv_b=c_conv_b, c_dt_bias=c_dt_bias,
                  c_a_log=c_a_log, c_d=c_d, c_norm=c_norm, w_br_a=w_br_a, w_br_b=w_br_b, w_br_c=w_br_c,
                  w_out=w_out, norm_x=norm_x, norm_mem=norm_mem, wx_q=wx_q, wx_k=wx_k, wx_v=wx_v, wx_o=wx_o,
                  norm_ffn=norm_ffn, w_up=w_up, f_conv_w=f_conv_w, f_conv_b=f_conv_b, w_down=w_down)
    nb, seq, _ = x_prompt.shape
    db, dseq, _ = x_sample.shape
    expand = _expand_matrix()
    gfin = norm_final[None, :]
    xp = x_prompt.reshape(nb * seq, D_MODEL)
    xs = x_sample.reshape(db * dseq, D_MODEL)
    p_states, s_states, p_mk, p_mv = [], [], [], []
    for l in range(DEPTH):
        lw = _prep_layer(l, params)
        last = gfin if l == DEPTH - 1 else None
        mk, mv = _memory_kv(mem_prompt, lw['norm_mem'], lw['wxk'], lw['wxv'])
        xp, st_p = _layer(
            xp, nb, seq, lw, None,
            jnp.zeros((nb, B_CONV - 1, 3 * B_WIDTH), F32),
            jnp.zeros((nb, B_HEADS, B_HEAD_DIM, B_HEAD_DIM), F32),
            jnp.zeros((nb, C_CONV - 1, C_XBC), F32),
            jnp.zeros((nb, C_HEADS, C_HEAD_DIM, C_STATE), F32),
            jnp.zeros((nb, F_CONV - 1, D_FF), F32),
            mk, mv, last, expand)
        p_states.append(st_p)
        p_mk.append(mk.reshape(nb, N_MEM, X_HEADS, X_HEAD_DIM))
        p_mv.append(mv.reshape(nb, N_MEM, X_HEADS, X_HEAD_DIM))
        xs, st_s = _layer(
            xs, db, dseq, lw, (cache_attn_k[l], cache_attn_v[l]),
            state_b_conv[l], state_b_rec[l], state_c_conv[l], state_c_ssm[l], state_ffn_conv[l],
            cache_mem_k[l].reshape(db, N_MEM, D_MODEL), cache_mem_v[l].reshape(db, N_MEM, D_MODEL),
            last, expand)
        s_states.append(st_s)

    y_prompt = xp.reshape(nb, seq, D_MODEL)
    y_sample = xs.reshape(db, dseq, D_MODEL)
    pst = [jnp.stack([s[i] for s in p_states]) for i in range(7)]
    sst = [jnp.stack([s[i] for s in s_states]) for i in range(7)]
    return (y_prompt, y_sample, pst[0], pst[1], pst[2], pst[3], pst[4], pst[5], pst[6],
            jnp.stack(p_mk), jnp.stack(p_mv),
            sst[0], sst[1], sst[2], sst[3], sst[4], sst[5], sst[6])
```

```python
import functools

import jax
import jax.numpy as jnp
import numpy as np
from jax import lax
from jax.experimental import pallas as pl
from jax.experimental.pallas import tpu as pltpu

F32 = jnp.float32
BF16 = jnp.bfloat16
SDS = jax.ShapeDtypeStruct

D_MODEL = 1024
DEPTH = 2
CHUNK = 64
N_MEM = 256
EPS = 1e-6

A_HEADS = 8
A_HEAD_DIM = 64
A_WIDTH = A_HEADS * A_HEAD_DIM
A_WINDOW = 8 * CHUNK
A_MAX_REL = 128

B_HEADS = 4
B_HEAD_DIM = 128
B_WIDTH = B_HEADS * B_HEAD_DIM
B_CONV = 4

C_HEADS = 16
C_HEAD_DIM = 64
C_INNER = C_HEADS * C_HEAD_DIM
C_GROUPS = 2
C_STATE = 128
C_XBC = C_INNER + 2 * C_GROUPS * C_STATE
C_CONV = 4
C_GROUP_W = C_INNER // C_GROUPS

X_HEADS = 4
X_HEAD_DIM = D_MODEL // X_HEADS

D_FF = 2816
F_CONV = 3

LANES = 128
SUBLANES = 8
NEG = -1e30
LOG2E = float(np.log2(np.e))
INPROJ_ROWS_PER_STEP = 1024
INPROJ_COLS_PER_STEP = 4608
INPROJ_CHUNK = 512
MERGE_ROWS_PER_STEP = 1024
MERGE_CHUNK = 512
XATTN_ROWS_PER_STEP = 1024
FFN_ROWS_PER_STEP = 1024
FFN_CHUNK = 256
FFN_DOWN_GROUP = 4
GDN_SEQS_PER_STEP = 4
GDN_ROWS_PER_STEP = 512
ATTN_HEADS_PER_STEP = 2
VMEM_LIMIT = 56 * 1024 * 1024

NZ = 9216
ZB_BQKV = 1
ZB_CZ = 3
ZB_CX = 4
ZB_CBC = 10
ZB_BGATE = 11
ZB_GATES = 6
SM_BETA = 0
SM_DEC = 4
SM_DT = 8

_O_BQKV = 3 * A_WIDTH
_O_BETA = _O_BQKV + 3 * B_WIDTH
_O_DEC = _O_BETA + B_HEADS
_O_BGATE = _O_DEC + B_HEADS
_O_CZ = _O_BGATE + B_WIDTH
_O_CXBC = _O_CZ + C_INNER
_O_CDT = _O_CXBC + C_XBC
_O_GATES = _O_CDT + C_HEADS
_SEGS_MAIN = ((0, _O_BETA), (_O_CZ, _O_CXBC), (_O_CXBC, _O_CDT), (_O_BGATE, _O_CZ), (_O_GATES, _O_GATES + 3 * D_MODEL))
_SEGS_SMALL = ((_O_BETA, _O_BGATE), (_O_CDT, _O_GATES))


def _cparams(sem):
    return pltpu.CompilerParams(dimension_semantics=sem, vmem_limit_bytes=VMEM_LIMIT)


def _rms(x, g):
    return x * lax.rsqrt(jnp.mean(x * x, axis=-1, keepdims=True) + EPS) * g


def _silu(x):
    return x * jax.nn.sigmoid(x)


def _softplus(x):
    return jnp.maximum(x, 0.0) + jnp.log1p(jnp.exp(-jnp.abs(x)))


def _dot(a, b):
    return jnp.dot(a, b, preferred_element_type=F32)


def _pieces(a, n):
    out = []
    for _ in range(n - 1):
        p = a.astype(BF16)
        out.append(p)
        a = a - p.astype(F32)
    out.append(a.astype(BF16))
    return out


def _dot_sel(sel, b):
    s16 = sel.astype(BF16)
    b1, b2, b3 = _pieces(b, 3)
    return _dot(s16, b1) + (_dot(s16, b2) + _dot(s16, b3))


def _dot_pick(a, sel):
    s16 = sel.astype(BF16)
    a1, a2, a3 = _pieces(a, 3)
    return _dot(a1, s16) + (_dot(a2, s16) + _dot(a3, s16))


def _dot_nt(a, b):
    return lax.dot_general(a, b, (((1,), (1,)), ((), ())), preferred_element_type=F32)


def _dot_tn(a, b):
    return lax.dot_general(a, b, (((0,), (0,)), ((), ())), preferred_element_type=F32)


def _row_start(i, n):
    return i * n if isinstance(i, int) else pl.multiple_of(i * n, n)


def _causal_conv(prev8, cur, w_ref, taps):
    ext = jnp.concatenate([prev8, cur], axis=0)
    acc = cur * w_ref[taps - 1:taps, :]
    for s in range(1, taps):
        acc = acc + pltpu.roll(ext, s, 0)[SUBLANES:, :] * w_ref[taps - 1 - s:taps - s, :]
    return acc, ext[cur.shape[0]:, :]


def _inproj_kernel(x_ref, g_ref, w_ref, ws_ref, z_ref, zs_ref, h_scr):
    @pl.when(pl.program_id(1) == 0)
    def _():
        hb = _rms(x_ref[...], g_ref[...]).astype(BF16)
        h_scr[...] = hb
        zs_ref[...] = _dot(hb, ws_ref[...])

    h = h_scr[...]
    for c in range(w_ref.shape[1] // INPROJ_CHUNK):
        cs = slice(c * INPROJ_CHUNK, (c + 1) * INPROJ_CHUNK)
        z_ref[:, cs] = _dot(h, w_ref[:, cs]).astype(BF16)


def _in_proj(x2d, g, w_main, w_small):
    T = x2d.shape[0]
    tm = min(INPROJ_ROWS_PER_STEP, T)
    tn = INPROJ_COLS_PER_STEP
    assert T % tm == 0 and NZ % tn == 0 and tn % INPROJ_CHUNK == 0
    return pl.pallas_call(
        _inproj_kernel,
        grid=(T // tm, NZ // tn),
        in_specs=[
            pl.BlockSpec((tm, D_MODEL), lambda i, j: (i, 0)),
            pl.BlockSpec((1, D_MODEL), lambda i, j: (0, 0)),
            pl.BlockSpec((D_MODEL, tn), lambda i, j: (0, j)),
            pl.BlockSpec((D_MODEL, LANES), lambda i, j: (0, 0)),
        ],
        out_specs=[
            pl.BlockSpec((tm, tn), lambda i, j: (i, j)),
            pl.BlockSpec((tm, LANES), lambda i, j: (i, 0)),
        ],
        out_shape=[SDS((T, NZ), BF16), SDS((T, LANES), F32)],
        scratch_shapes=[pltpu.VMEM((tm, D_MODEL), BF16)],
        compiler_params=_cparams(("parallel", "arbitrary")),
        name="in_proj",
    )(x2d, g, w_main, w_small)


def _attn_kernel(*refs, L, C, G, has_past):
    if has_past:
        q_ref, k_ref, v_ref, pk_ref, pv_ref, bias_ref, o_ref, kx, vx = refs
        kx[0:A_WINDOW, :] = pk_ref[0].astype(BF16)
        vx[0:A_WINDOW, :] = pv_ref[0].astype(BF16)
    else:
        q_ref, k_ref, v_ref, bias_ref, o_ref, kx, vx = refs
        kx[0:A_WINDOW, :] = jnp.zeros((A_WINDOW, A_WIDTH), BF16)
        vx[0:A_WINDOW, :] = jnp.zeros((A_WINDOW, A_WIDTH), BF16)
    kx[A_WINDOW:A_WINDOW + L, :] = k_ref[...]
    vx[A_WINDOW:A_WINDOW + L, :] = v_ref[...]

    GC = G * C
    NB = A_WINDOW + GC
    lane = lax.broadcasted_iota(jnp.int32, (1, LANES), 1)
    first_head = lane < A_HEAD_DIM
    col = lax.broadcasted_iota(jnp.int32, (1, NB), 1)
    scale = A_HEAD_DIM ** -0.5

    def group(g, carry, masked):
        r0 = _row_start(g, GC)
        q = q_ref[pl.ds(r0, GC), :] * jnp.asarray(scale, BF16)
        kb = kx[pl.ds(r0, NB), :]
        vb = vx[pl.ds(r0, NB), :]
        sl = [slice((h // 2) * LANES, (h // 2 + 1) * LANES) for h in range(A_HEADS)]
        batches = [range(h0, h0 + ATTN_HEADS_PER_STEP) for h0 in range(0, A_HEADS, ATTN_HEADS_PER_STEP)]

        def scores(hs):
            msk = {h: first_head if h % 2 == 0 else jnp.logical_not(first_head) for h in hs}
            s = {h: _dot_nt(jnp.where(msk[h], q[:, sl[h]], jnp.zeros((GC, LANES), BF16)), kb[:, sl[h]]) for h in hs}
            s = {h: s[h] + bias_ref[h] for h in hs}
            if masked:
                s = {h: jnp.where(r0 + col >= A_WINDOW, s[h], NEG) for h in hs}
            return s

        def softmax(hs, s):
            p = {h: jnp.exp(s[h] - jnp.max(s[h], axis=-1, keepdims=True)) for h in hs}
            return p, {h: jnp.sum(p[h], axis=-1, keepdims=True) for h in hs}

        def values(hs, p, l):
            pv = {h: _dot(p[h].astype(BF16), vb[:, sl[h]]) / l[h] for h in hs}
            for h in hs:
                if h % 2 == 1:
                    o_ref[pl.ds(r0, GC), sl[h]] = jnp.where(first_head, pv[h - 1], pv[h]).astype(BF16)

        s = scores(batches[0])
        for i, hs in enumerate(batches):
            p, l = softmax(hs, s)
            if i + 1 < len(batches):
                s = scores(batches[i + 1])
            values(hs, p, l)
        return carry

    ng = L // GC
    n_masked = 0 if has_past else min(ng, -(-A_WINDOW // GC))
    if ng == 1:
        group(0, 0, n_masked > 0)
    else:
        if n_masked:
            lax.fori_loop(0, n_masked, functools.partial(group, masked=True), 0)
        if ng > n_masked:
            lax.fori_loop(n_masked, ng, functools.partial(group, masked=False), 0)


def _attn_bias(table, C, G):
    GC = G * C
    NB = A_WINDOW + GC
    r = np.arange(GC)[:, None]
    j = np.arange(NB)[None, :]
    lo = (r // C) * C
    allowed = (j >= lo) & (j < lo + A_WINDOW + C)
    dmax = A_WINDOW + GC - 1
    rel = np.clip(dmax - np.arange(NB + GC - 1), -A_MAX_REL, A_MAX_REL) + A_MAX_REL
    vflip = table[rel].astype(F32).T
    M = NB + GC - 1
    flat = jnp.tile(jnp.pad(vflip, ((0, 0), (0, 1))), (1, GC))[:, :GC * M]
    b = flat.reshape(A_HEADS, GC, M)[:, :, GC - 1:GC - 1 + NB]
    return jnp.where(allowed[None], b, NEG)


def _band_attn(z, bsz, L, C, G, bias, past):
    T = bsz * L
    has_past = past is not None
    NB = A_WINDOW + G * C
    in_specs = [pl.BlockSpec((L, A_WIDTH), lambda b, i=i: (b, i)) for i in range(3)]
    args = [z, z, z]
    if has_past:
        in_specs += [pl.BlockSpec((1, A_WINDOW, A_WIDTH), lambda b: (b, 0, 0))] * 2
        args += list(past)
    in_specs.append(pl.BlockSpec((A_HEADS, G * C, NB), lambda b: (0, 0, 0)))
    args.append(bias)
    return pl.pallas_call(
        functools.partial(_attn_kernel, L=L, C=C, G=G, has_past=has_past),
        grid=(bsz,),
        in_specs=in_specs,
        out_specs=pl.BlockSpec((L, A_WIDTH), lambda b: (b, 0)),
        out_shape=SDS((T, A_WIDTH), BF16),
        scratch_shapes=[pltpu.VMEM((A_WINDOW + L, A_WIDTH), BF16)] * 2,
        compiler_params=_cparams(("parallel",)),
        name="band_attn",
    )(*args)


def _gdn_kernel(qkv_ref, sm_ref, gate_ref, cst_ref, rst_ref, cw_ref, prm_ref, bn_ref,
                yb_ref, cst_o_ref, rst_o_ref, s_scr, prev_scr, *, nb, lb, cl):
    t = pl.program_id(1)
    nc = lb // cl

    @pl.when(t == 0)
    def _():
        for n in range(nb):
            for h in range(B_HEADS):
                s_scr[n * B_HEADS + h] = rst_ref[n, h]
        prev_scr[...] = cst_ref[...]

    bias_row = prm_ref[0:1, :]
    aneg_row = -jnp.exp(prm_ref[1:2, :])
    bn = bn_ref[...]
    tril = (lax.broadcasted_iota(jnp.int32, (cl, cl), 0) >= lax.broadcasted_iota(jnp.int32, (cl, cl), 1)).astype(F32)
    ri = lax.broadcasted_iota(jnp.int32, (cl, 2 * cl), 0)
    lane2 = lax.broadcasted_iota(jnp.int32, (cl, 2 * cl), 1)
    ci = lane2 & (cl - 1)
    first = lane2 < cl
    incl = ri >= ci
    strict = ri > ci
    eye = (ri == ci).astype(F32)
    sh = int(np.log2(SUBLANES))
    diag_blk = (ri >> sh) == (ci >> sh)
    sub_blk = []
    while (1 << sh) < cl:
        sub_blk.append(((ri >> (sh + 1)) == (ci >> (sh + 1))) & (((ri >> sh) & 1) == 1) & (((ci >> sh) & 1) == 0))
        sh += 1
    U = [(n, h) for n in range(nb) for h in range(B_HEADS)]
    PR = [(U[i], U[i + 1]) for i in range(0, len(U), 2)]
    zk = jnp.zeros((cl, B_HEAD_DIM), BF16)
    zr = jnp.zeros((cl, 2 * B_HEAD_DIM), BF16)

    def bdiag(x):
        z = jnp.zeros_like(x)
        return jnp.concatenate([jnp.where(first, x, z), jnp.where(first, z, x)], axis=0)

    def split_bd(x):
        hi, lo = _pieces(x, 2)
        return bdiag(hi), bdiag(lo)

    def x3(a, bd_hi, bd_lo):
        ah, al = _pieces(a, 2)
        both = _dot(jnp.concatenate([ah, al], axis=0), bd_hi)
        return both[:cl] + (both[cl:] + _dot(ah, bd_lo))

    def body(c, carry):
        r0 = _row_start(c, cl)
        rows = pl.ds(r0, cl)
        act, beta_all, gv = [], [], []
        for n in range(nb):
            conv, new_prev = _causal_conv(prev_scr[n], qkv_ref[n, rows, :].astype(F32), cw_ref, B_CONV)
            prev_scr[n] = new_prev
            act.append(_silu(conv))
            smc = sm_ref[n, rows, :]
            beta_all.append(jax.nn.sigmoid(smc))
            gv.append(_softplus(smc + bias_row) * aneg_row)
        gcs = [_dot_sel(tril, gv[n]) for n in range(nb)]
        q = {(n, h): act[n][:, h * B_HEAD_DIM:(h + 1) * B_HEAD_DIM] for n, h in U}
        k = {(n, h): act[n][:, B_WIDTH + h * B_HEAD_DIM:B_WIDTH + (h + 1) * B_HEAD_DIM] for n, h in U}
        v = {(n, h): act[n][:, 2 * B_WIDTH + h * B_HEAD_DIM:2 * B_WIDTH + (h + 1) * B_HEAD_DIM] for n, h in U}
        q = {u: q[u] * lax.rsqrt(jnp.sum(q[u] * q[u], axis=-1, keepdims=True) + EPS) * (B_HEAD_DIM ** -0.5) for u in U}
        k = {u: k[u] * lax.rsqrt(jnp.sum(k[u] * k[u], axis=-1, keepdims=True) + EPS) for u in U}
        beta = {(n, h): beta_all[n][:, SM_BETA + h:SM_BETA + h + 1] for n, h in U}
        g = {(n, h): gv[n][:, SM_DEC + h:SM_DEC + h + 1] for n, h in U}
        gc = {(n, h): gcs[n][:, SM_DEC + h:SM_DEC + h + 1] for n, h in U}
        gl = {(n, h): gcs[n][cl - 1:cl, SM_DEC + h:SM_DEC + h + 1] for n, h in U}
        e = {pr: _dot_sel(tril, jnp.where(strict, jnp.where(first, g[pr[0]], g[pr[1]]), 0.0)) for pr in PR}
        decay = {pr: jnp.where(incl, jnp.exp(e[pr]), 0.0) for pr in PR}
        kb = {u: k[u] * beta[u] for u in U}
        k16 = {u: k[u].astype(BF16) for u in U}
        kpad = {pr: (jnp.concatenate([k16[pr[0]], zk], axis=0), jnp.concatenate([zk, k16[pr[1]]], axis=0)) for pr in PR}
        a = {pr: _dot_nt(kb[pr[0]].astype(BF16), kpad[pr][0]) + _dot_nt(kb[pr[1]].astype(BF16), kpad[pr][1])
             for pr in PR}
        lm = {pr: jnp.where(strict, a[pr] * decay[pr], 0.0) for pr in PR}
        m = {pr: jnp.where(diag_blk, -lm[pr], 0.0) for pr in PR}
        p = {pr: eye + m[pr] for pr in PR}
        for _ in range(2):
            mb = {pr: split_bd(m[pr]) for pr in PR}
            m = {pr: x3(m[pr], *mb[pr]) for pr in PR}
            mb = {pr: split_bd(m[pr]) for pr in PR}
            p = {pr: p[pr] + x3(p[pr], *mb[pr]) for pr in PR}
        for cmask in sub_blk:
            pb = {pr: split_bd(p[pr]) for pr in PR}
            cx = {pr: x3(jnp.where(cmask, lm[pr], 0.0), *pb[pr]) for pr in PR}
            cb = {pr: split_bd(cx[pr]) for pr in PR}
            p = {pr: p[pr] - x3(p[pr], *cb[pr]) for pr in PR}
        egc = {u: jnp.exp(gc[u]) for u in U}
        rhs = {u: _pieces(jnp.concatenate([v[u] * beta[u], kb[u] * egc[u]], axis=1), 2) for u in U}
        rb = {pr: [jnp.concatenate([jnp.concatenate([rhs[pr[0]][i], zr], axis=1),
                                    jnp.concatenate([zr, rhs[pr[1]][i]], axis=1)], axis=0) for i in range(2)]
              for pr in PR}
        solp = {pr: x3(p[pr], *rb[pr]) for pr in PR}
        sol = {}
        for pr in PR:
            sol[pr[0]] = solp[pr][:, :2 * B_HEAD_DIM]
            sol[pr[1]] = solp[pr][:, 2 * B_HEAD_DIM:]
        qk = {pr: ((_dot_nt(q[pr[0]].astype(BF16), kpad[pr][0]) + _dot_nt(q[pr[1]].astype(BF16), kpad[pr][1]))
                   * decay[pr]).astype(BF16) for pr in PR}
        s0 = {(n, h): s_scr[n * B_HEADS + h] for n, h in U}
        s16 = {u: s0[u].astype(BF16) for u in U}
        ws = {u: _dot(jnp.concatenate([sol[u][:, B_HEAD_DIM:].astype(BF16), (q[u] * egc[u]).astype(BF16)], axis=0),
                      s16[u]) for u in U}
        uu = {u: sol[u][:, :B_HEAD_DIM] - ws[u][:cl] for u in U}
        u16 = {u: uu[u].astype(BF16) for u in U}
        kt = {u: (k[u] * jnp.exp(gl[u] - gc[u])).astype(BF16) for u in U}
        snew = {u: s0[u] * jnp.exp(gl[u]) + _dot_tn(kt[u], u16[u]) for u in U}
        o = {}
        for pr in PR:
            ustack = jnp.concatenate([u16[pr[0]], u16[pr[1]]], axis=0)
            zq = jnp.zeros_like(qk[pr])
            intra = _dot(jnp.concatenate([jnp.where(first, qk[pr], zq), jnp.where(first, zq, qk[pr])], axis=0), ustack)
            o[pr[0]] = ws[pr[0]][cl:] + intra[:cl]
            o[pr[1]] = ws[pr[1]][cl:] + intra[cl:]
        for n, h in U:
            hs = slice(h * B_HEAD_DIM, (h + 1) * B_HEAD_DIM)
            s_scr[n * B_HEADS + h] = snew[n, h]
            gate = gate_ref[n, rows, hs].astype(F32)
            yb_ref[n, rows, hs] = (_rms(o[n, h], bn) * _silu(gate)).astype(BF16)
        return carry

    if nc == 1:
        body(0, 0)
    else:
        lax.fori_loop(0, nc, body, 0)

    @pl.when(t == pl.num_programs(1) - 1)
    def _():
        cst_o_ref[...] = prev_scr[:, SUBLANES - (B_CONV - 1):, :]
        for n in range(nb):
            for h in range(B_HEADS):
                rst_o_ref[n, h] = s_scr[n * B_HEADS + h]


def _gdn(z, zs, bsz, L, cst, rst, cw, prm, bn):
    cl = min(CHUNK, L)
    nb = GDN_SEQS_PER_STEP
    lb = min(GDN_ROWS_PER_STEP, L)
    assert bsz % nb == 0 and L % lb == 0 and lb % cl == 0
    z3 = z.reshape(bsz, L, NZ)
    zs3 = zs.reshape(bsz, L, LANES)
    yb, cst_o, rst_o = pl.pallas_call(
        functools.partial(_gdn_kernel, nb=nb, lb=lb, cl=cl),
        grid=(bsz // nb, L // lb),
        in_specs=[
            pl.BlockSpec((nb, lb, 3 * B_WIDTH), lambda b, t: (b, t, ZB_BQKV)),
            pl.BlockSpec((nb, lb, LANES), lambda b, t: (b, t, 0)),
            pl.BlockSpec((nb, lb, B_WIDTH), lambda b, t: (b, t, ZB_BGATE)),
            pl.BlockSpec((nb, SUBLANES, 3 * B_WIDTH), lambda b, t: (b, 0, 0)),
            pl.BlockSpec((nb, B_HEADS, B_HEAD_DIM, B_HEAD_DIM), lambda b, t: (b, 0, 0, 0)),
            pl.BlockSpec((B_CONV, 3 * B_WIDTH), lambda b, t: (0, 0)),
            pl.BlockSpec((SUBLANES, LANES), lambda b, t: (0, 0)),
            pl.BlockSpec((1, B_HEAD_DIM), lambda b, t: (0, 0)),
        ],
        out_specs=[
            pl.BlockSpec((nb, lb, B_WIDTH), lambda b, t: (b, t, 0)),
            pl.BlockSpec((nb, B_CONV - 1, 3 * B_WIDTH), lambda b, t: (b, 0, 0)),
            pl.BlockSpec((nb, B_HEADS, B_HEAD_DIM, B_HEAD_DIM), lambda b, t: (b, 0, 0, 0)),
        ],
        out_shape=[
            SDS((bsz, L, B_WIDTH), BF16),
            SDS((bsz, B_CONV - 1, 3 * B_WIDTH), F32),
            SDS((bsz, B_HEADS, B_HEAD_DIM, B_HEAD_DIM), F32),
        ],
        scratch_shapes=[
            pltpu.VMEM((nb * B_HEADS, B_HEAD_DIM, B_HEAD_DIM), F32),
            pltpu.VMEM((nb, SUBLANES, 3 * B_WIDTH), F32),
        ],
        compiler_params=_cparams(("parallel", "arbitrary")),
        name="gdn",
    )(z3, zs3, z3, cst, rst, cw, prm, bn)
    return yb.reshape(bsz * L, B_WIDTH), cst_o, rst_o


def _ssd_kernel(cx_ref, cbc_ref, cz_ref, sm_ref, cst_ref, sst_ref, cwx_ref, cwbc_ref, cbx_ref, cbbc_ref,
                expand_ref, prm_ref, rows_ref, yc_ref, cst_o_ref, sst_o_ref, h_scr, *, L, cl):
    nc = L // cl
    h_scr[...] = sst_ref[0]
    bias_row = prm_ref[0:1, :]
    aneg_row = -jnp.exp(prm_ref[1:2, :])
    cdx = rows_ref[0:1, :]
    cn = rows_ref[1:2, :]
    ri = lax.broadcasted_iota(jnp.int32, (cl, cl), 0)
    ci = lax.broadcasted_iota(jnp.int32, (cl, cl), 1)
    tril = (ri >= ci).astype(F32)
    rx = lax.broadcasted_iota(jnp.int32, (cl, C_INNER), 0)
    jx = lax.broadcasted_iota(jnp.int32, (cl, C_INNER), 1) & (C_HEAD_DIM - 1)
    inclx = rx >= jx
    lane = lax.broadcasted_iota(jnp.int32, (1, LANES), 1)
    first_head = lane < C_HEAD_DIM

    def pad_rows(a):
        if cl == CHUNK:
            return a
        return jnp.concatenate([a, jnp.zeros((CHUNK - cl, a.shape[1]), a.dtype)], axis=0)

    def pad_lanes(a):
        if cl == C_HEAD_DIM:
            return a
        return jnp.concatenate([a, jnp.zeros((a.shape[0], C_HEAD_DIM - cl), a.dtype)], axis=1)

    def body(c, carry):
        px, pbc = carry
        r0 = _row_start(c, cl)
        convx, npx = _causal_conv(px, cx_ref[pl.ds(r0, cl), :].astype(F32), cwx_ref, C_CONV)
        convbc, npbc = _causal_conv(pbc, cbc_ref[pl.ds(r0, cl), :].astype(F32), cwbc_ref, C_CONV)
        xs = _silu(convx + cbx_ref[...])
        bcs = _silu(convbc + cbbc_ref[...])
        dt_c = _softplus(sm_ref[pl.ds(r0, cl), :] + bias_row)
        ac_c = _dot_sel(tril, dt_c * aneg_row) * LOG2E
        dtx = _dot_pick(dt_c, expand_ref[...])
        acx = _dot_pick(ac_c, expand_ref[...])
        ac_t = ac_c.T
        rowsj = [pad_lanes(ac_t[SM_DT + h:SM_DT + h + 1, :]) for h in range(C_HEADS)]
        ac_row = jnp.concatenate(rowsj, axis=1)
        decayx = jnp.exp2(jnp.where(inclx, acx - ac_row, NEG))
        alast = acx[cl - 1:cl, :]
        eac = jnp.exp2(acx)
        ealast = jnp.exp2(alast)
        xdt = xs * dtx
        xtil = (xdt * jnp.exp2(alast - acx)).astype(BF16)
        ys = []
        for g in range(C_GROUPS):
            gs = slice(g * C_GROUP_W, (g + 1) * C_GROUP_W)
            bg = bcs[:, g * C_STATE:(g + 1) * C_STATE].astype(BF16)
            cg = bcs[:, (C_GROUPS + g) * C_STATE:(C_GROUPS + g + 1) * C_STATE].astype(BF16)
            brep = jnp.concatenate([pad_rows(bg)] * (C_GROUP_W // CHUNK), axis=0)
            w = (_dot_nt(cg, brep) * decayx[:, gs]).astype(BF16)
            hg = h_scr[g]
            yoff = _dot(cg, hg.astype(BF16)) * eac[:, gs]
            yd = []
            for pr in range(C_GROUP_W // LANES):
                lo = g * C_GROUP_W + pr * LANES
                xp = pad_rows(xdt[:, lo:lo + LANES])
                bd = jnp.concatenate([jnp.where(first_head, xp, 0.0), jnp.where(first_head, 0.0, xp)], axis=0)
                yd.append(_dot(w[:, pr * LANES:(pr + 1) * LANES], bd.astype(BF16)))
            h_scr[g] = hg * ealast[:, gs] + _dot_tn(bg, xtil[:, gs])
            ys.append(jnp.concatenate(yd, axis=1) + yoff)
        y = jnp.concatenate(ys, axis=1) + cdx * xs
        t = y * _silu(cz_ref[pl.ds(r0, cl), :].astype(F32))
        yc_ref[pl.ds(r0, cl), :] = _rms(t, cn).astype(BF16)
        return npx, npbc

    carry = (cst_ref[0, :, :C_INNER], cst_ref[0, :, C_INNER:])
    carry = body(0, carry) if nc == 1 else lax.fori_loop(0, nc, body, carry)
    cst_o_ref[0, :, :C_INNER] = carry[0][SUBLANES - (C_CONV - 1):, :]
    cst_o_ref[0, :, C_INNER:] = carry[1][SUBLANES - (C_CONV - 1):, :]
    sst_o_ref[0] = h_scr[...]


def _ssd(z, zs, bsz, L, cst, sst, cw, cb, expand, prm, rows):
    T = bsz * L
    cl = min(CHUNK, L)
    nbc = C_XBC - C_INNER
    full = lambda shape: pl.BlockSpec(shape, lambda b: (0,) * len(shape))
    return pl.pallas_call(
        functools.partial(_ssd_kernel, L=L, cl=cl),
        grid=(bsz,),
        in_specs=[
            pl.BlockSpec((L, C_INNER), lambda b: (b, ZB_CX)),
            pl.BlockSpec((L, nbc), lambda b: (b, ZB_CBC)),
            pl.BlockSpec((L, C_INNER), lambda b: (b, ZB_CZ)),
            pl.BlockSpec((L, LANES), lambda b: (b, 0)),
            pl.BlockSpec((1, SUBLANES, C_XBC), lambda b: (b, 0, 0)),
            pl.BlockSpec((1, C_GROUPS, C_STATE, C_GROUP_W), lambda b: (b, 0, 0, 0)),
            full((C_CONV, C_INNER)),
            full((C_CONV, nbc)),
            full((1, C_INNER)),
            full((1, nbc)),
            full((LANES, C_INNER)),
            full((SUBLANES, LANES)),
            full((SUBLANES, C_INNER)),
        ],
        out_specs=[
            pl.BlockSpec((L, C_INNER), lambda b: (b, 0)),
            pl.BlockSpec((1, C_CONV - 1, C_XBC), lambda b: (b, 0, 0)),
            pl.BlockSpec((1, C_GROUPS, C_STATE, C_GROUP_W), lambda b: (b, 0, 0, 0)),
        ],
        out_shape=[
            SDS((T, C_INNER), BF16),
            SDS((bsz, C_CONV - 1, C_XBC), F32),
            SDS((bsz, C_GROUPS, C_STATE, C_GROUP_W), F32),
        ],
        scratch_shapes=[pltpu.VMEM((C_GROUPS, C_STATE, C_GROUP_W), F32)],
        compiler_params=_cparams(("parallel",)),
        name="ssd",
    )(z, z, z, zs, cst, sst, cw[:, :C_INNER], cw[:, C_INNER:], cb[:, :C_INNER], cb[:, C_INNER:], expand, prm, rows)


def _merge_kernel(x_ref, ya_ref, yb_ref, yc_ref, ga_ref, gb_ref, gc_ref, wa_ref, wb_ref, wc_ref, wo_ref, o_ref):
    ya, yb, yc = ya_ref[...], yb_ref[...], yc_ref[...]
    ms = []
    for lo in range(0, D_MODEL, MERGE_CHUNK):
        cs = slice(lo, lo + MERGE_CHUNK)
        m = jax.nn.sigmoid(ga_ref[:, cs].astype(F32)) * _dot(ya, wa_ref[:, cs])
        m = m + jax.nn.sigmoid(gb_ref[:, cs].astype(F32)) * _dot(yb, wb_ref[:, cs])
        m = m + jax.nn.sigmoid(gc_ref[:, cs].astype(F32)) * _dot(yc, wc_ref[:, cs])
        ms.append(m.astype(BF16))
    o_ref[...] = x_ref[...] + _dot(jnp.concatenate(ms, axis=1), wo_ref[...])


def _merge(x2d, ya, yb, yc, z, wa, wb, wc, wo):
    T = x2d.shape[0]
    tm = min(MERGE_ROWS_PER_STEP, T)
    assert T % tm == 0
    row = lambda w, cb=0: pl.BlockSpec((tm, w), lambda i: (i, cb))
    full = lambda shape: pl.BlockSpec(shape, lambda i: (0, 0), pipeline_mode=pl.Buffered(1))
    return pl.pallas_call(
        _merge_kernel,
        grid=(T // tm,),
        in_specs=[
            row(D_MODEL), row(A_WIDTH), row(B_WIDTH), row(C_INNER),
            row(D_MODEL, ZB_GATES), row(D_MODEL, ZB_GATES + 1), row(D_MODEL, ZB_GATES + 2),
            full((A_WIDTH, D_MODEL)), full((B_WIDTH, D_MODEL)), full((C_INNER, D_MODEL)), full((D_MODEL, D_MODEL)),
        ],
        out_specs=row(D_MODEL),
        out_shape=SDS((T, D_MODEL), F32),
        compiler_params=_cparams(("parallel",)),
        name="merge",
    )(x2d, ya, yb, yc, z, z, z, wa, wb, wc, wo)


def _memkv_kernel(m_ref, g_ref, wk_ref, wv_ref, k_ref, v_ref):
    hm = _rms(m_ref[0], g_ref[...]).astype(BF16)
    k_ref[0] = _dot(hm, wk_ref[...])
    v_ref[0] = _dot(hm, wv_ref[...])


def _memory_kv(mem, g, wk, wv):
    bsz = mem.shape[0]
    blk = pl.BlockSpec((1, N_MEM, D_MODEL), lambda b: (b, 0, 0))
    full = lambda shape: pl.BlockSpec(shape, lambda b: (0, 0))
    return pl.pallas_call(
        _memkv_kernel,
        grid=(bsz,),
        in_specs=[blk, full((1, D_MODEL)), full((D_MODEL, D_MODEL)), full((D_MODEL, D_MODEL))],
        out_specs=[blk, blk],
        out_shape=[SDS((bsz, N_MEM, D_MODEL), F32)] * 2,
        compiler_params=_cparams(("parallel",)),
        name="memory_kv",
    )(mem, g, wk, wv)


def _xattn_kernel(x_ref, mk_ref, mv_ref, g_ref, wq_ref, wo_ref, o_ref, mk16, mv16):
    @pl.when(pl.program_id(1) == 0)
    def _():
        mk16[...] = mk_ref[0].astype(BF16)
        mv16[...] = mv_ref[0].astype(BF16)

    x = x_ref[...]
    hq = _rms(x, g_ref[...]).astype(BF16)
    q = (_dot(hq, wq_ref[...]) * (X_HEAD_DIM ** -0.5)).astype(BF16)
    H = range(X_HEADS)
    hs = [slice(h * X_HEAD_DIM, (h + 1) * X_HEAD_DIM) for h in H]
    s = [_dot_nt(q[:, hs[h]], mk16[:, hs[h]]) for h in H]
    p = [jnp.exp(s[h] - jnp.max(s[h], axis=-1, keepdims=True)) for h in H]
    l = [jnp.sum(p[h], axis=-1, keepdims=True) for h in H]
    o = [(_dot(p[h].astype(BF16), mv16[:, hs[h]]) / l[h]).astype(BF16) for h in H]
    acc = x
    for h in H:
        acc = acc + _dot(o[h], wo_ref[hs[h], :])
    o_ref[...] = acc


def _xattn(x2d, bsz, L, mk, mv, g, wq, wo):
    tq = min(XATTN_ROWS_PER_STEP, L)
    assert L % tq == 0
    nq = L // tq
    row = pl.BlockSpec((tq, D_MODEL), lambda b, t: (b * nq + t, 0))
    mem = pl.BlockSpec((1, N_MEM, D_MODEL), lambda b, t: (b, 0, 0))
    full = lambda shape: pl.BlockSpec(shape, lambda b, t: (0, 0))
    return pl.pallas_call(
        _xattn_kernel,
        grid=(bsz, nq),
        in_specs=[row, mem, mem, full((1, D_MODEL)), full((D_MODEL, D_MODEL)), full((D_MODEL, D_MODEL))],
        out_specs=row,
        out_shape=SDS((bsz * L, D_MODEL), F32),
        scratch_shapes=[pltpu.VMEM((N_MEM, D_MODEL), BF16)] * 2,
        compiler_params=_cparams(("parallel", "arbitrary")),
        name="xattn",
    )(x2d, mk, mv, g, wq, wo)


def _ffn_kernel(*refs, final_norm):
    if final_norm:
        x_ref, g_ref, wu_ref, wd_ref, cw_ref, cb_ref, fst_ref, gf_ref, o_ref, fst_o_ref, halo_scr = refs
    else:
        x_ref, g_ref, wu_ref, wd_ref, cw_ref, cb_ref, fst_ref, o_ref, fst_o_ref, halo_scr = refs

    @pl.when(pl.program_id(1) == 0)
    def _():
        halo_scr[...] = fst_ref[0]

    x = x_ref[...]
    hf = _rms(x, g_ref[...]).astype(BF16)
    acc = None
    pending = []
    for lo in range(0, D_FF, FFN_CHUNK):
        hi = min(lo + FFN_CHUNK, D_FF)
        cs = slice(lo, hi)
        u = _dot(hf, wu_ref[:, cs])
        gpre = _dot(hf, wu_ref[:, D_FF + lo:D_FF + hi])
        conv, new_halo = _causal_conv(halo_scr[:, cs], gpre, cw_ref[:, cs], F_CONV)
        halo_scr[:, cs] = new_halo
        pending.append((lo, hi, (u * _silu(conv + cb_ref[:, cs])).astype(BF16)))
        if len(pending) == FFN_DOWN_GROUP or hi == D_FF:
            act = jnp.concatenate([a for _, _, a in pending], axis=1)
            contrib = _dot(act, wd_ref[pending[0][0]:pending[-1][1], :])
            acc = contrib if acc is None else acc + contrib
            pending = []
    fst_o_ref[0, 0] = halo_scr[...]
    y = x + acc
    o_ref[...] = _rms(y, gf_ref[...]) if final_norm else y


def _ffn(x2d, bsz, L, g, wup, wdn, cw, cb, fst, gfinal):
    tr = min(FFN_ROWS_PER_STEP, L)
    nt = L // tr
    assert L % tr == 0 and FFN_CHUNK % LANES == 0
    final_norm = gfinal is not None
    row = pl.BlockSpec((tr, D_MODEL), lambda b, t: (b * nt + t, 0))
    const = lambda shape: pl.BlockSpec(shape, lambda b, t: (0, 0), pipeline_mode=pl.Buffered(1))
    in_specs = [
        row, const((1, D_MODEL)), const((D_MODEL, 2 * D_FF)), const((D_FF, D_MODEL)),
        const((F_CONV, D_FF)), const((1, D_FF)),
        pl.BlockSpec((1, SUBLANES, D_FF), lambda b, t: (b, 0, 0)),
    ]
    args = [x2d, g, wup, wdn, cw, cb, fst]
    if final_norm:
        in_specs.append(const((1, D_MODEL)))
        args.append(gfinal)
    return pl.pallas_call(
        functools.partial(_ffn_kernel, final_norm=final_norm),
        grid=(bsz, nt),
        in_specs=in_specs,
        out_specs=[row, pl.BlockSpec((1, 1, SUBLANES, D_FF), lambda b, t: (b, t, 0, 0))],
        out_shape=[SDS((bsz * L, D_MODEL), F32), SDS((bsz, nt, SUBLANES, D_FF), F32)],
        scratch_shapes=[pltpu.VMEM((SUBLANES, D_FF), F32)],
        compiler_params=_cparams(("parallel", "arbitrary")),
        name="ffn",
    )(*args)


def _pad_state(st):
    return jnp.pad(st.astype(F32), ((0, 0), (SUBLANES - st.shape[1], 0), (0, 0)))


def _head_row(v, width):
    return jnp.repeat(v.astype(F32), width)[None, :]


def _prep_layer(l, p):
    w_in = p['w_in'][l]
    w_main = jnp.concatenate([w_in[:, a:b].astype(BF16) for a, b in _SEGS_MAIN], axis=1)
    n_small = sum(b - a for a, b in _SEGS_SMALL)
    w_small = jnp.concatenate([w_in[:, a:b] for a, b in _SEGS_SMALL] + [jnp.zeros((D_MODEL, LANES - n_small), F32)],
                              axis=1).astype(BF16)
    prm = jnp.zeros((SUBLANES, LANES), F32)
    prm = prm.at[0, SM_DEC:SM_DEC + B_HEADS].set(p['b_dt_bias'][l])
    prm = prm.at[0, SM_DT:SM_DT + C_HEADS].set(p['c_dt_bias'][l])
    prm = prm.at[1, SM_DEC:SM_DEC + B_HEADS].set(p['b_a_log'][l])
    prm = prm.at[1, SM_DT:SM_DT + C_HEADS].set(p['c_a_log'][l])
    rows = jnp.zeros((SUBLANES, C_INNER), F32)
    rows = rows.at[0].set(_head_row(p['c_d'][l], C_HEAD_DIM)[0])
    rows = rows.at[1].set(p['c_norm'][l].astype(F32))
    return dict(
        norm_mix=p['norm_mix'][l][None, :], w_main=w_main, w_small=w_small,
        rel=p['a_rel_bias'][l],
        b_conv_w=p['b_conv_w'][l], prm=prm, b_norm=p['b_norm'][l][None, :],
        c_conv_w=p['c_conv_w'][l], c_conv_b=p['c_conv_b'][l][None, :], rows=rows,
        wa=p['w_br_a'][l].astype(BF16), wb=p['w_br_b'][l].astype(BF16), wc=p['w_br_c'][l].astype(BF16),
        wo=p['w_out'][l].astype(BF16),
        norm_x=p['norm_x'][l][None, :], norm_mem=p['norm_mem'][l][None, :],
        wxq=p['wx_q'][l].astype(BF16), wxk=p['wx_k'][l].astype(BF16), wxv=p['wx_v'][l].astype(BF16),
        wxo=p['wx_o'][l].astype(BF16),
        norm_ffn=p['norm_ffn'][l][None, :], w_up=p['w_up'][l].astype(BF16), w_down=p['w_down'][l].astype(BF16),
        f_conv_w=p['f_conv_w'][l], f_conv_b=p['f_conv_b'][l][None, :],
    )


def _expand_matrix():
    e = np.zeros((LANES, C_INNER), np.float32)
    for h in range(C_HEADS):
        e[SM_DT + h, h * C_HEAD_DIM:(h + 1) * C_HEAD_DIM] = 1.0
    return jnp.asarray(e)


def _layer(x2d, bsz, L, lw, a_past, b_conv, b_rec, c_conv, c_ssm, f_conv, mk, mv, gfinal, expand):
    z, zs = _in_proj(x2d, lw['norm_mix'], lw['w_main'], lw['w_small'])

    C = min(CHUNK, L)
    G = 2 if L // C >= 2 else 1
    bias = _attn_bias(lw['rel'], C, G)
    past = None
    if a_past is not None:
        past = tuple(t.reshape(bsz, A_WINDOW, A_WIDTH) for t in a_past)
    ya = _band_attn(z, bsz, L, C, G, bias, past)
    keep = min(A_WINDOW, L)
    z3 = z.reshape(bsz, L, NZ)
    a_k = z3[:, L - keep:, A_WIDTH:2 * A_WIDTH].astype(F32).reshape(bsz, keep, A_HEADS, A_HEAD_DIM)
    a_v = z3[:, L - keep:, 2 * A_WIDTH:3 * A_WIDTH].astype(F32).reshape(bsz, keep, A_HEADS, A_HEAD_DIM)

    yb, b_conv_new, b_rec_new = _gdn(z, zs, bsz, L, _pad_state(b_conv), b_rec.astype(F32),
                                     lw['b_conv_w'], lw['prm'], lw['b_norm'])

    sst = jnp.transpose(c_ssm.astype(F32).reshape(bsz, C_GROUPS, C_HEADS // C_GROUPS, C_HEAD_DIM, C_STATE),
                        (0, 1, 4, 2, 3)).reshape(bsz, C_GROUPS, C_STATE, C_GROUP_W)
    yc, c_conv_new, sst_new = _ssd(z, zs, bsz, L, _pad_state(c_conv), sst, lw['c_conv_w'], lw['c_conv_b'],
                                   expand, lw['prm'], lw['rows'])
    c_ssm_new = jnp.transpose(sst_new.reshape(bsz, C_GROUPS, C_STATE, C_HEADS // C_GROUPS, C_HEAD_DIM),
                              (0, 1, 3, 4, 2)).reshape(bsz, C_HEADS, C_HEAD_DIM, C_STATE)

    x2d = _merge(x2d, ya, yb, yc, z, lw['wa'], lw['wb'], lw['wc'], lw['wo'])
    x2d = _xattn(x2d, bsz, L, mk, mv, lw['norm_x'], lw['wxq'], lw['wxo'])

    x2d, fst_new = _ffn(x2d, bsz, L, lw['norm_ffn'], lw['w_up'], lw['w_down'], lw['f_conv_w'], lw['f_conv_b'],
                        _pad_state(f_conv), gfinal)
    f_conv_new = fst_new[:, -1, SUBLANES - (F_CONV - 1):, :]
    return x2d, (a_k, a_v, b_conv_new, b_rec_new, c_conv_new, c_ssm_new, f_conv_new)


def kernel(x_prompt, x_sample, cache_attn_k, cache_attn_v, state_b_conv, state_b_rec, state_c_conv, state_c_ssm, state_ffn_conv, cache_mem_k, cache_mem_v, mem_prompt, norm_mix, w_in, a_rel_bias, b_conv_w, b_a_log, b_dt_bias, b_norm, c_conv_w, c_conv_b, c_dt_bias, c_a_log, c_d, c_norm, w_br_a, w_br_b, w_br_c, w_out, norm_x, norm_mem, wx_q, wx_k, wx_v, wx_o, norm_ffn, w_up, f_conv_w, f_conv_b, w_down, norm_final):
    params = dict(norm_mix=norm_mix, w_in=w_in, a_rel_bias=a_rel_bias, b_conv_w=b_conv_w, b_a_log=b_a_log,
                  b_dt_bias=b_dt_bias, b_norm=b_norm, c_conv_w=c_conv_w, c_conv_b=c_conv_b, c_dt_bias=c_dt_bias,
                  c_a_log=c_a_log, c_d=c_d, c_norm=c_norm, w_br_a=w_br_a, w_br_b=w_br_b, w_br_c=w_br_c,
                  w_out=w_out, norm_x=norm_x, norm_mem=norm_mem, wx_q=wx_q, wx_k=wx_k, wx_v=wx_v, wx_o=wx_o,
                  norm_ffn=norm_ffn, w_up=w_up, f_conv_w=f_conv_w, f_conv_b=f_conv_b, w_down=w_down)
    nb, seq, _ = x_prompt.shape
    db, dseq, _ = x_sample.shape
    expand = _expand_matrix()
    gfin = norm_final[None, :]
    xp = x_prompt.reshape(nb * seq, D_MODEL)
    xs = x_sample.reshape(db * dseq, D_MODEL)
    p_states, s_states, p_mk, p_mv = [], [], [], []
    for l in range(DEPTH):
        lw = _prep_layer(l, params)
        last = gfin if l == DEPTH - 1 else None
        mk, mv = _memory_kv(mem_prompt, lw['norm_mem'], lw['wxk'], lw['wxv'])
        xp, st_p = _layer(
            xp, nb, seq, lw, None,
            jnp.zeros((nb, B_CONV - 1, 3 * B_WIDTH), F32),
            jnp.zeros((nb, B_HEADS, B_HEAD_DIM, B_HEAD_DIM), F32),
            jnp.zeros((nb, C_CONV - 1, C_XBC), F32),
            jnp.zeros((nb, C_HEADS, C_HEAD_DIM, C_STATE), F32),
            jnp.zeros((nb, F_CONV - 1, D_FF), F32),
            mk, mv, last, expand)
        p_states.append(st_p)
        p_mk.append(mk.reshape(nb, N_MEM, X_HEADS, X_HEAD_DIM))
        p_mv.append(mv.reshape(nb, N_MEM, X_HEADS, X_HEAD_DIM))
        xs, st_s = _layer(
            xs, db, dseq, lw, (cache_attn_k[l], cache_attn_v[l]),
            state_b_conv[l], state_b_rec[l], state_c_conv[l], state_c_ssm[l], state_ffn_conv[l],
            cache_mem_k[l].reshape(db, N_MEM, D_MODEL), cache_mem_v[l].reshape(db, N_MEM, D_MODEL),
            last, expand)
        s_states.append(st_s)

    y_prompt = xp.reshape(nb, seq, D_MODEL)
    y_sample = xs.reshape(db, dseq, D_MODEL)
    pst = [jnp.stack([s[i] for s in p_states]) for i in range(7)]
    sst = [jnp.stack([s[i] for s in s_states]) for i in range(7)]
    return (y_prompt, y_sample, pst[0], pst[1], pst[2], pst[3], pst[4], pst[5], pst[6],
            jnp.stack(p_mk), jnp.stack(p_mv),
            sst[0], sst[1], sst[2], sst[3], sst[4], sst[5], sst[6])
```
